```python
import jax
import jax.numpy as jnp
from jax import lax
import numpy as np

D_MODEL = 4096
BATCH = 2
SEQ = 4096
DEPTH = 1
DEC_BATCH = 32
DEC_SEQ = 64
PAST_LEN = 2048

CHUNK = 64
Q_BLOCK = 128
SB_HEADS = 16
SB_HEAD_DIM = D_MODEL // 32
D_SB = SB_HEADS * SB_HEAD_DIM
ML_HEADS = 8
ML_DV = D_MODEL // 16
ML_DK = ML_DV // 2
D_ML = ML_HEADS * ML_DV
D_MLK = ML_HEADS * ML_DK
N_EXPERTS = 256
TOP_K = 8
N_GROUPS = 8
TOPK_GROUPS = 4
D_EXPERT = D_MODEL // 8
D_SHARED = D_EXPERT
ROUTED_SCALE = 2.5
MOE_BLOCK = 128
N_MOD = 6
EPS = 1e-6
IN_SPLITS = (D_SB, D_SB, D_SB, D_MLK, D_MLK, D_ML, D_ML, ML_HEADS, ML_HEADS, D_MODEL, D_MODEL)
D_IN = sum(IN_SPLITS)

kernel_name = 'hybrid_sb_mlstm_moe_stream_step'


def _in_cuts():
    cuts, acc = [], 0
    for width in IN_SPLITS[:-1]:
        acc += width
        cuts.append(acc)
    return cuts


def rmsnorm(x, w):
    xf = x.astype(jnp.float32)
    y = xf * lax.rsqrt(jnp.mean(xf * xf, axis=-1, keepdims=True) + EPS)
    return (y * w.astype(jnp.float32)).astype(x.dtype)


def modulate(x, w, shift, scale):
    return rmsnorm(x, w) * (1 + scale[:, None, :]) + shift[:, None, :]


def ada_mod(c, w_ada, b_ada):
    mod = jax.nn.silu(c) @ w_ada + b_ada
    return jnp.split(mod, N_MOD, axis=-1)


def in_projection(xm, w_in, b_ig, b_fg):
    B, T, _ = xm.shape
    h = jnp.einsum('btd,de->bte', xm, w_in)
    q_sb, k_sb, v_sb, q_ml, k_ml, v_ml, o_ml, i_ml, f_ml, g_sb, g_ml = jnp.split(h, _in_cuts(), axis=-1)
    q_sb = q_sb.reshape(B, T, SB_HEADS, SB_HEAD_DIM)
    k_sb = k_sb.reshape(B, T, SB_HEADS, SB_HEAD_DIM)
    v_sb = v_sb.reshape(B, T, SB_HEADS, SB_HEAD_DIM)
    q_ml = q_ml.reshape(B, T, ML_HEADS, ML_DK)
    k_ml = k_ml.reshape(B, T, ML_HEADS, ML_DK) * (ML_DK ** -0.5)
    v_ml = v_ml.reshape(B, T, ML_HEADS, ML_DV)
    i_ml = (i_ml + b_ig).astype(jnp.float32)
    logf_ml = jax.nn.log_sigmoid((f_ml + b_fg).astype(jnp.float32))
    return q_sb, k_sb, v_sb, q_ml, k_ml, v_ml, o_ml, i_ml, logf_ml, g_sb, g_ml


def sb_attend(q, k, v, q_pos, k_pos):
    z = jnp.einsum('bqhd,bkhd->bhqk', q.astype(jnp.float32), k.astype(jnp.float32)) * (SB_HEAD_DIM ** -0.5)
    mask = k_pos[None, :] < q_pos[:, None]
    sp = jnp.where(mask, jax.nn.softplus(z), 0.0)
    incl = lax.cumsum(sp, axis=3, reverse=True)
    excl = jnp.concatenate([incl[..., 1:], jnp.zeros_like(incl[..., :1])], axis=-1)
    a = jnp.where(mask, jnp.exp(jax.nn.log_sigmoid(z) - excl), 0.0)
    return jnp.einsum('bhqk,bkhd->bqhd', a, v.astype(jnp.float32)).astype(v.dtype)


def sb_prompt(q, k, v):
    B, S = q.shape[0], q.shape[1]
    nb = S // Q_BLOCK
    qb = q.reshape(B, nb, Q_BLOCK, SB_HEADS, SB_HEAD_DIM).transpose(1, 0, 2, 3, 4)
    pos = jnp.arange(S, dtype=jnp.int32)
    pb = pos.reshape(nb, Q_BLOCK)
    out = lax.map(lambda blk: sb_attend(blk[0], k, v, blk[1], pos), (qb, pb))
    return out.transpose(1, 0, 2, 3, 4).reshape(B, S, SB_HEADS, SB_HEAD_DIM)


def mlstm_chunkwise(q, k, v, ig, logf, c0, n0, m0, chunk):
    B, T = q.shape[0], q.shape[1]
    nc = T // chunk
    f32 = jnp.float32
    qc = q.astype(f32).reshape(B, nc, chunk, ML_HEADS, ML_DK).transpose(1, 0, 3, 2, 4)
    kc = k.astype(f32).reshape(B, nc, chunk, ML_HEADS, ML_DK).transpose(1, 0, 3, 2, 4)
    vc = v.astype(f32).reshape(B, nc, chunk, ML_HEADS, ML_DV).transpose(1, 0, 3, 2, 4)
    ic = ig.reshape(B, nc, chunk, ML_HEADS).transpose(1, 0, 3, 2)
    fc = logf.reshape(B, nc, chunk, ML_HEADS).transpose(1, 0, 3, 2)
    tril = jnp.tril(jnp.ones((chunk, chunk), dtype=bool))

    def step(carry, blk):
        c, n, m = carry
        qb, kb, vb, ib, fb = blk
        b = jnp.cumsum(fb, axis=-1)
        d_log = jnp.where(tril, b[..., :, None] - b[..., None, :] + ib[..., None, :], -jnp.inf)
        inter = b + m[..., None]
        m_t = jnp.maximum(jnp.max(d_log, axis=-1), inter)
        w_intra = jnp.exp(d_log - m_t[..., None]) * jnp.einsum('bhtd,bhsd->bhts', qb, kb)
        w_inter = jnp.exp(inter - m_t)
        num = jnp.einsum('bhts,bhsv->bhtv', w_intra, vb) + w_inter[..., None] * jnp.einsum('bhtd,bhdv->bhtv', qb, c)
        den = jnp.sum(w_intra, axis=-1) + w_inter * jnp.einsum('bhtd,bhd->bht', qb, n)
        h = num / jnp.maximum(jnp.abs(den), jnp.exp(-m_t))[..., None]
        b_last = b[..., -1]
        log_src = b_last[..., None] - b + ib
        m_new = jnp.maximum(b_last + m, jnp.max(log_src, axis=-1))
        src_w = jnp.exp(log_src - m_new[..., None])
        car_w = jnp.exp(b_last + m - m_new)
        c_new = car_w[..., None, None] * c + jnp.einsum('bhs,bhsd,bhsv->bhdv', src_w, kb, vb)
        n_new = car_w[..., None] * n + jnp.einsum('bhs,bhsd->bhd', src_w, kb)
        return (c_new, n_new, m_new), h

    (c, n, m), h = lax.scan(step, (c0.astype(f32), n0.astype(f32), m0.astype(f32)), (qc, kc, vc, ic, fc))
    h = h.transpose(1, 0, 3, 2, 4).reshape(B, T, ML_HEADS, ML_DV)
    return h, c, n, m


def head_rmsnorm(h, w):
    B, T = h.shape[0], h.shape[1]
    hf = h.astype(jnp.float32)
    hn = hf * lax.rsqrt(jnp.mean(hf * hf, axis=-1, keepdims=True) + EPS)
    return hn.reshape(B, T, D_ML) * w.astype(jnp.float32)


def merge_branches(y_sb, h_ml, o_ml, g_sb, g_ml, ml_norm_w, w_branch, w_o):
    B, T = y_sb.shape[0], y_sb.shape[1]
    dtype = o_ml.dtype
    y_ml = (head_rmsnorm(h_ml, ml_norm_w) * jax.nn.sigmoid(o_ml.astype(jnp.float32))).astype(dtype)
    u_sb = jnp.einsum('bte,ed->btd', y_sb.reshape(B, T, D_SB), w_branch[0])
    u_ml = jnp.einsum('bte,ed->btd', y_ml, w_branch[1])
    merged = jax.nn.sigmoid(g_sb) * u_sb + jax.nn.sigmoid(g_ml) * u_ml
    return jnp.einsum('btd,de->bte', merged, w_o)


def route(x2d, w_router, router_bias):
    T = x2d.shape[0]
    s = jax.nn.sigmoid((x2d @ w_router).astype(jnp.float32))
    sb = s + router_bias.astype(jnp.float32)
    grp = sb.reshape(T, N_GROUPS, N_EXPERTS // N_GROUPS)
    gscore = jnp.sum(lax.top_k(grp, 2)[0], axis=-1)
    _, gidx = lax.top_k(gscore, TOPK_GROUPS)
    gmask = jnp.any(gidx[..., None] == jnp.arange(N_GROUPS), axis=1)
    emask = jnp.repeat(gmask, N_EXPERTS // N_GROUPS, axis=1)
    _, eidx = lax.top_k(jnp.where(emask, sb, -jnp.inf), TOP_K)
    w = jnp.take_along_axis(s, eidx, axis=1)
    w = w / jnp.sum(w, axis=-1, keepdims=True) * ROUTED_SCALE
    return eidx, w


def moe(x2d, w_router, router_bias, w_e_gate, w_e_up, w_e_down, w_s_gate, w_s_up, w_s_down):
    T, D = x2d.shape
    eidx, gw = route(x2d, w_router, router_bias)
    A = T * TOP_K
    flat_e = eidx.reshape(-1).astype(jnp.int32)
    flat_tok = jnp.repeat(jnp.arange(T, dtype=jnp.int32), TOP_K)
    flat_w = gw.reshape(-1)
    order = jnp.argsort(flat_e)
    e_sorted, tok_sorted, w_sorted = flat_e[order], flat_tok[order], flat_w[order]
    counts = jax.ops.segment_sum(jnp.ones((A,), jnp.int32), flat_e, num_segments=N_EXPERTS)
    cstart = jnp.cumsum(counts) - counts
    pcounts = (counts + MOE_BLOCK - 1) // MOE_BLOCK * MOE_BLOCK
    pend = jnp.cumsum(pcounts)
    pstart = pend - pcounts
    dest = pstart[e_sorted] + jnp.arange(A, dtype=jnp.int32) - cstart[e_sorted]
    P = (A + MOE_BLOCK - 1) // MOE_BLOCK * MOE_BLOCK + N_EXPERTS * MOE_BLOCK
    nb = P // MOE_BLOCK
    tok_buf = jnp.full((P,), T, jnp.int32).at[dest].set(tok_sorted)
    w_buf = jnp.zeros((P,), jnp.float32).at[dest].set(w_sorted)
    blk_e = jnp.minimum(jnp.searchsorted(pend, jnp.arange(nb, dtype=jnp.int32) * MOE_BLOCK, side='right'), N_EXPERTS - 1)
    xz = jnp.concatenate([x2d, jnp.zeros((1, D), x2d.dtype)], axis=0)

    def body(y, blk):
        idx, wb, e = blk
        xb = xz[idx]
        hh = jax.nn.silu(xb @ w_e_gate[e]) * (xb @ w_e_up[e])
        out = (hh @ w_e_down[e]) * wb[:, None].astype(x2d.dtype)
        return y.at[idx].add(out), None

    y, _ = lax.scan(body, jnp.zeros((T + 1, D), x2d.dtype),
                    (tok_buf.reshape(nb, MOE_BLOCK), w_buf.reshape(nb, MOE_BLOCK), blk_e))
    shared = (jax.nn.silu(x2d @ w_s_gate) * (x2d @ w_s_up)) @ w_s_down
    return y[:T] + shared


def setup_inputs(seed: int = 0) -> dict:
    key = jax.random.key(seed)
    ks = jax.random.split(key, 32)
    nrm = jax.random.normal
    f32 = jnp.float32
    L = DEPTH
    return {
        'x_prompt': nrm(ks[0], (BATCH, SEQ, D_MODEL), f32),
        'x_sample': nrm(ks[1], (DEC_BATCH, DEC_SEQ, D_MODEL), f32),
        'cache_sb_k': nrm(ks[2], (L, DEC_BATCH, PAST_LEN, SB_HEADS, SB_HEAD_DIM), f32),
        'cache_sb_v': nrm(ks[3], (L, DEC_BATCH, PAST_LEN, SB_HEADS, SB_HEAD_DIM), f32),
        'state_mlstm_c': 0.3 * nrm(ks[4], (L, DEC_BATCH, ML_HEADS, ML_DK, ML_DV), f32),
        'state_mlstm_n': 0.3 * nrm(ks[5], (L, DEC_BATCH, ML_HEADS, ML_DK), f32),
        'state_mlstm_m': nrm(ks[6], (L, DEC_BATCH, ML_HEADS), f32),
        'c_prompt': nrm(ks[7], (BATCH, D_MODEL), f32),
        'c_sample': nrm(ks[8], (DEC_BATCH, D_MODEL), f32),
        'w_ada': 0.5 * D_MODEL ** -0.5 * nrm(ks[9], (L, D_MODEL, N_MOD * D_MODEL), f32),
        'b_ada': 0.02 * nrm(ks[10], (L, N_MOD * D_MODEL), f32),
        'norm1_w': 1.0 + 0.02 * nrm(ks[11], (L, D_MODEL), f32),
        'w_in': D_MODEL ** -0.5 * nrm(ks[12], (L, D_MODEL, D_IN), f32),
        'b_ig': 0.1 * nrm(ks[13], (L, ML_HEADS), f32),
        'b_fg': 3.0 + 3.0 * jax.random.uniform(ks[14], (L, ML_HEADS), f32),
        'ml_norm_w': 1.0 + 0.02 * nrm(ks[15], (L, D_ML), f32),
        'w_branch': D_SB ** -0.5 * nrm(ks[16], (L, 2, D_SB, D_MODEL), f32),
        'w_o': D_MODEL ** -0.5 * nrm(ks[17], (L, D_MODEL, D_MODEL), f32),
        'norm2_w': 1.0 + 0.02 * nrm(ks[18], (L, D_MODEL), f32),
        'w_router': D_MODEL ** -0.5 * nrm(ks[19], (L, D_MODEL, N_EXPERTS), f32),
        'router_bias': 0.01 * nrm(ks[20], (L, N_EXPERTS), f32),
        'w_e_gate': D_MODEL ** -0.5 * nrm(ks[21], (L, N_EXPERTS, D_MODEL, D_EXPERT), f32),
        'w_e_up': D_MODEL ** -0.5 * nrm(ks[22], (L, N_EXPERTS, D_MODEL, D_EXPERT), f32),
        'w_e_down': D_EXPERT ** -0.5 * nrm(ks[23], (L, N_EXPERTS, D_EXPERT, D_MODEL), f32),
        'w_s_gate': D_MODEL ** -0.5 * nrm(ks[24], (L, D_MODEL, D_SHARED), f32),
        'w_s_up': D_MODEL ** -0.5 * nrm(ks[25], (L, D_MODEL, D_SHARED), f32),
        'w_s_down': D_SHARED ** -0.5 * nrm(ks[26], (L, D_SHARED, D_MODEL), f32),
        'norm_f_w': 1.0 + 0.02 * nrm(ks[27], (D_MODEL,), f32),
    }


def reference(x_prompt, x_sample, cache_sb_k, cache_sb_v, state_mlstm_c, state_mlstm_n, state_mlstm_m,
              c_prompt, c_sample, w_ada, b_ada, norm1_w, w_in, b_ig, b_fg, ml_norm_w, w_branch, w_o,
              norm2_w, w_router, router_bias, w_e_gate, w_e_up, w_e_down, w_s_gate, w_s_up, w_s_down,
              norm_f_w):
    xp, xs = x_prompt, x_sample
    Bp, Sp = xp.shape[0], xp.shape[1]
    Bs, Ss = xs.shape[0], xs.shape[1]
    past = cache_sb_k.shape[2]
    q_pos_s = past + jnp.arange(Ss, dtype=jnp.int32)
    k_pos_s = jnp.arange(past + Ss, dtype=jnp.int32)
    kp_l, vp_l, cp_l, np_l, mp_l = [], [], [], [], []
    ks_l, vs_l, cs_l, ns_l, ms_l = [], [], [], [], []
    for l in range(DEPTH):
        sh1p, sc1p, g1p, sh2p, sc2p, g2p = ada_mod(c_prompt, w_ada[l], b_ada[l])
        sh1s, sc1s, g1s, sh2s, sc2s, g2s = ada_mod(c_sample, w_ada[l], b_ada[l])

        q_sb, k_sb, v_sb, q_ml, k_ml, v_ml, o_ml, i_ml, f_ml, g_sb, g_ml = in_projection(
            modulate(xp, norm1_w[l], sh1p, sc1p), w_in[l], b_ig[l], b_fg[l])
        y_sb = sb_prompt(q_sb, k_sb, v_sb)
        h_ml, c_p, n_p, m_p = mlstm_chunkwise(
            q_ml, k_ml, v_ml, i_ml, f_ml,
            jnp.zeros((Bp, ML_HEADS, ML_DK, ML_DV), jnp.float32),
            jnp.zeros((Bp, ML_HEADS, ML_DK), jnp.float32),
            jnp.zeros((Bp, ML_HEADS), jnp.float32), CHUNK)
        xp = xp + g1p[:, None, :] * merge_branches(y_sb, h_ml, o_ml, g_sb, g_ml, ml_norm_w[l], w_branch[l], w_o[l])
        kp_l.append(k_sb)
        vp_l.append(v_sb)
        cp_l.append(c_p.astype(state_mlstm_c.dtype))
        np_l.append(n_p.astype(state_mlstm_n.dtype))
        mp_l.append(m_p.astype(state_mlstm_m.dtype))

        q_sb, k_sb, v_sb, q_ml, k_ml, v_ml, o_ml, i_ml, f_ml, g_sb, g_ml = in_projection(
            modulate(xs, norm1_w[l], sh1s, sc1s), w_in[l], b_ig[l], b_fg[l])
        k_all = jnp.concatenate([cache_sb_k[l], k_sb], axis=1)
        v_all = jnp.concatenate([cache_sb_v[l], v_sb], axis=1)
        y_sb = sb_attend(q_sb, k_all, v_all, q_pos_s, k_pos_s)
        h_ml, c_s, n_s, m_s = mlstm_chunkwise(
            q_ml, k_ml, v_ml, i_ml, f_ml, state_mlstm_c[l], state_mlstm_n[l], state_mlstm_m[l], Ss)
        xs = xs + g1s[:, None, :] * merge_branches(y_sb, h_ml, o_ml, g_sb, g_ml, ml_norm_w[l], w_branch[l], w_o[l])
        ks_l.append(k_sb)
        vs_l.append(v_sb)
        cs_l.append(c_s.astype(state_mlstm_c.dtype))
        ns_l.append(n_s.astype(state_mlstm_n.dtype))
        ms_l.append(m_s.astype(state_mlstm_m.dtype))

        hp = modulate(xp, norm2_w[l], sh2p, sc2p).reshape(Bp * Sp, D_MODEL)
        hs = modulate(xs, norm2_w[l], sh2s, sc2s).reshape(Bs * Ss, D_MODEL)
        ff = moe(jnp.concatenate([hp, hs], axis=0), w_router[l], router_bias[l], w_e_gate[l], w_e_up[l],
                 w_e_down[l], w_s_gate[l], w_s_up[l], w_s_down[l])
        xp = xp + g2p[:, None, :] * ff[:Bp * Sp].reshape(Bp, Sp, D_MODEL)
        xs = xs + g2s[:, None, :] * ff[Bp * Sp:].reshape(Bs, Ss, D_MODEL)

    y_prompt = rmsnorm(xp, norm_f_w)
    y_sample = rmsnorm(xs, norm_f_w)
    return (y_prompt, y_sample,
            jnp.stack(kp_l), jnp.stack(vp_l), jnp.stack(cp_l), jnp.stack(np_l), jnp.stack(mp_l),
            jnp.stack(ks_l), jnp.stack(vs_l), jnp.stack(cs_l), jnp.stack(ns_l), jnp.stack(ms_l))
```

```python
import functools

import jax
import jax.numpy as jnp
from jax import lax
from jax.experimental import pallas as pl
from jax.experimental.pallas import tpu as pltpu

F32 = jnp.float32
BF16 = jnp.bfloat16
U32 = jnp.uint32
I32 = jnp.int32

EPS = 1e-6
GROUP = 64
HEAD = 128
SB_HEADS = 16
ML_HEADS = 8
ML_DV = 256
N_GROUPS = 8
TOPK_GROUPS = 4
TOP_K = 8
ROUTED_SCALE = 2.5
MOE_BLOCK = 128
SB_TQ = 256
SB_TK = 128
SB_CACHE_BLOCK = 512
VMEM_LIMIT = 56 * 1024 * 1024


def _cp(*sem):
    return pltpu.CompilerParams(dimension_semantics=sem, vmem_limit_bytes=VMEM_LIMIT)


def _pick(n, prefs):
    for p in prefs:
        if n % p == 0:
            return p
    return n


def _ada_kernel(c_ref, w_ref, b_ref, o_ref):
    c = c_ref[...]
    a = (c * jax.nn.sigmoid(c)).astype(BF16)
    o_ref[...] = jnp.dot(a, w_ref[...].astype(BF16), preferred_element_type=F32) + b_ref[...]


def _ada(c, w, b):
    m, k = c.shape
    n = w.shape[1]
    tn = _pick(n, (512, 256, 128))
    return pl.pallas_call(
        _ada_kernel,
        grid=(n // tn,),
        in_specs=[pl.BlockSpec((m, k), lambda j: (0, 0)),
                  pl.BlockSpec((k, tn), lambda j: (0, j)),
                  pl.BlockSpec((1, tn), lambda j: (0, j))],
        out_specs=pl.BlockSpec((m, tn), lambda j: (0, j)),
        out_shape=jax.ShapeDtypeStruct((m, n), F32),
        compiler_params=_cp("parallel"),
        name="ada",
    )(c, w, b.reshape(1, n))


def _modulated(x, nw, sc, sh):
    tm, d = x.shape
    ms = jnp.mean(x * x, axis=-1, keepdims=True)
    y = x * lax.rsqrt(ms + EPS) * nw
    y3 = y.reshape(tm // GROUP, GROUP, d) * (1.0 + sc) + sh
    return y3.reshape(tm, d)


def _modnorm_kernel(x_ref, nw_ref, sc_ref, sh_ref, o_ref):
    o_ref[...] = _modulated(x_ref[...], nw_ref[...], sc_ref[...], sh_ref[...]).astype(o_ref.dtype)


def _modnorm(x, nw, sc, sh):
    t, d = x.shape
    tm = _pick(t, (256, 128, 64))
    g = tm // GROUP
    return pl.pallas_call(
        _modnorm_kernel,
        grid=(t // tm,),
        in_specs=[pl.BlockSpec((tm, d), lambda i: (i, 0)),
                  pl.BlockSpec((1, d), lambda i: (0, 0)),
                  pl.BlockSpec((g, 1, d), lambda i: (i, 0, 0)),
                  pl.BlockSpec((g, 1, d), lambda i: (i, 0, 0))],
        out_specs=pl.BlockSpec((tm, d), lambda i: (i, 0)),
        out_shape=jax.ShapeDtypeStruct((t, d), BF16),
        compiler_params=_cp("parallel"),
        name="modnorm1",
    )(x, nw.reshape(1, d), sc, sh)


def _mm_kernel(a_ref, w_ref, o_ref):
    o_ref[...] = jnp.dot(a_ref[...], w_ref[...].astype(BF16),
                         preferred_element_type=F32).astype(o_ref.dtype)


def _matmul(a, w, n, name):
    m, k = a.shape
    tm = _pick(m, (1024, 512, 384, 256, 128))
    tn = _pick(n, (512, 640, 256, 128))
    return pl.pallas_call(
        _mm_kernel,
        grid=(m // tm, n // tn),
        in_specs=[pl.BlockSpec((tm, k), lambda i, j: (i, 0)),
                  pl.BlockSpec((k, tn), lambda i, j: (0, j))],
        out_specs=pl.BlockSpec((tm, tn), lambda i, j: (i, j)),
        out_shape=jax.ShapeDtypeStruct((m, n), F32),
        compiler_params=_cp("parallel", "parallel"),
        name=name,
    )(a, w)


def _softplus(z):
    return jnp.maximum(z, 0.0) + jnp.log1p(jnp.exp(-jnp.abs(z)))


def _sb_step(q, k, v, carry, acc, tri, mask):
    z = lax.dot_general(q, k, (((1,), (1,)), ((), ())), preferred_element_type=F32) * (HEAD ** -0.5)
    sp = _softplus(z)
    spm = sp if mask is None else jnp.where(mask, sp, 0.0)
    hi = spm.astype(BF16)
    mid = (spm - hi.astype(F32)).astype(BF16)
    excl = (jnp.dot(hi, tri, preferred_element_type=F32)
            + jnp.dot(mid, tri, preferred_element_type=F32) + carry)
    a = jnp.exp(z - sp - excl)
    if mask is not None:
        a = jnp.where(mask, a, 0.0)
    acc = acc + jnp.dot(a.astype(BF16), v, preferred_element_type=F32)
    carry = carry + jnp.sum(spm, axis=1, keepdims=True)
    return carry, acc


def _tri(n):
    r = lax.broadcasted_iota(I32, (n, n), 0)
    c = lax.broadcasted_iota(I32, (n, n), 1)
    return jnp.where(r > c, 1.0, 0.0).astype(BF16)


def _sb_prompt_kernel(q_ref, k_ref, v_ref, o_ref):
    qi = pl.program_id(2)
    tq, tk = SB_TQ, SB_TK
    q = q_ref[...].astype(BF16)
    tri = _tri(tk)
    qpos = lax.broadcasted_iota(I32, (tq, tk), 0)
    kpos = lax.broadcasted_iota(I32, (tq, tk), 1)

    def load(kb):
        start = pl.multiple_of(kb * tk, tk)
        return k_ref[pl.ds(start, tk), :].astype(BF16), v_ref[pl.ds(start, tk), :].astype(BF16)

    carry = jnp.zeros((tq, 1), F32)
    acc = jnp.zeros((tq, HEAD), F32)
    for d in range(tq // tk - 1, -1, -1):
        kb = qi * (tq // tk) + d
        k, v = load(kb)
        carry, acc = _sb_step(q, k, v, carry, acc, tri, kpos + d * tk < qpos)

    def body(i, ca):
        kb = qi * (tq // tk) - 1 - i
        k, v = load(kb)
        return _sb_step(q, k, v, ca[0], ca[1], tri, None)

    carry, acc = lax.fori_loop(0, qi * (tq // tk), body, (carry, acc))
    o_ref[...] = acc.astype(o_ref.dtype)


def _sb_prompt(main, batch, seq):
    nq = seq // SB_TQ
    return pl.pallas_call(
        _sb_prompt_kernel,
        grid=(batch, SB_HEADS, nq),
        in_specs=[pl.BlockSpec((SB_TQ, HEAD), lambda b, h, i: (b * nq + i, h)),
                  pl.BlockSpec((seq, HEAD), lambda b, h, i: (b, SB_HEADS + h)),
                  pl.BlockSpec((seq, HEAD), lambda b, h, i: (b, 2 * SB_HEADS + h))],
        out_specs=pl.BlockSpec((SB_TQ, HEAD), lambda b, h, i: (b * nq + i, h)),
        out_shape=jax.ShapeDtypeStruct((batch * seq, SB_HEADS * HEAD), BF16),
        compiler_params=_cp("parallel", "parallel", "arbitrary"),
        name="sb_prompt",
    )(main, main, main)


def _sb_sample_kernel(q_ref, kn_ref, vn_ref, kc_ref, vc_ref, o_ref, carry_ref, acc_ref):
    s = pl.program_id(1)
    ns = pl.num_programs(1)
    lq = q_ref.shape[0]

    @pl.when(s == 0)
    def _():
        tri = _tri(lq)
        mask = (lax.broadcasted_iota(I32, (lq, lq), 1) < lax.broadcasted_iota(I32, (lq, lq), 0))

        def head(h, _):
            cols = pl.ds(pl.multiple_of(h * HEAD, HEAD), HEAD)
            q = q_ref[:, cols].astype(BF16)
            k = kn_ref[:, cols].astype(BF16)
            v = vn_ref[:, cols].astype(BF16)
            carry, acc = _sb_step(q, k, v, jnp.zeros((lq, 1), F32), jnp.zeros((lq, HEAD), F32), tri, mask)
            carry_ref[h] = carry
            acc_ref[h] = acc
            return 0

        lax.fori_loop(0, SB_HEADS, head, 0)

    @pl.when(s > 0)
    def _():
        tri = _tri(SB_TK)
        nsub = kc_ref.shape[0] // SB_TK

        def head(h, _):
            cols = pl.ds(pl.multiple_of(h * HEAD, HEAD), HEAD)
            q = q_ref[:, cols].astype(BF16)
            carry = carry_ref[h]
            acc = acc_ref[h]
            for j in range(nsub - 1, -1, -1):
                k = kc_ref[j * SB_TK:(j + 1) * SB_TK, cols].astype(BF16)
                v = vc_ref[j * SB_TK:(j + 1) * SB_TK, cols].astype(BF16)
                carry, acc = _sb_step(q, k, v, carry, acc, tri, None)
            carry_ref[h] = carry
            acc_ref[h] = acc
            return 0

        lax.fori_loop(0, SB_HEADS, head, 0)

    @pl.when(s == ns - 1)
    def _():
        for h in range(SB_HEADS):
            o_ref[:, h * HEAD:(h + 1) * HEAD] = acc_ref[h].astype(o_ref.dtype)


def _sb_sample(main, cache_k, cache_v, row_block0, batch, lq):
    past = cache_k.shape[1]
    cb = _pick(past, (SB_CACHE_BLOCK, 256, 128))
    ncb = past // cb
    dsb = SB_HEADS * HEAD

    def cache_map(b, s):
        return (b, jnp.maximum(ncb - s, 0) - jnp.where(s == 0, 1, 0), 0)

    return pl.pallas_call(
        _sb_sample_kernel,
        grid=(batch, ncb + 1),
        in_specs=[pl.BlockSpec((lq, dsb), lambda b, s: (row_block0 + b, 0)),
                  pl.BlockSpec((lq, dsb), lambda b, s: (row_block0 + b, 1)),
                  pl.BlockSpec((lq, dsb), lambda b, s: (row_block0 + b, 2)),
                  pl.BlockSpec((None, cb, dsb), cache_map),
                  pl.BlockSpec((None, cb, dsb), cache_map)],
        out_specs=pl.BlockSpec((lq, dsb), lambda b, s: (b, 0)),
        out_shape=jax.ShapeDtypeStruct((batch * lq, dsb), BF16),
        scratch_shapes=[pltpu.VMEM((SB_HEADS, lq, 1), F32),
                        pltpu.VMEM((SB_HEADS, lq, HEAD), F32)],
        compiler_params=_cp("parallel", "arbitrary"),
        name="sb_sample",
    )(main, main, main, cache_k, cache_v)


def _split3(x):
    p0 = x.astype(BF16)
    r = x - p0.astype(F32)
    p1 = r.astype(BF16)
    p2 = (r - p1.astype(F32)).astype(BF16)
    return p0, p1, p2


def _mlstm_kernel(q_ref, k_ref, v_ref, o_ref, if_ref, bias_ref, nw_ref, c0_ref, n0_ref, m0_ref,
                  y_ref, c_s, n_s, m_s):
    ci = pl.program_id(1)
    L = GROUP

    @pl.when(ci == 0)
    def _():
        c_s[...] = c0_ref[...]
        n_s[...] = n0_ref[...]
        m_s[...] = m0_ref[...]

    g = if_ref[...] + bias_ref[...]
    logf = pltpu.roll(jax.nn.log_sigmoid(g), HEAD - ML_HEADS, axis=1)
    row = lax.broadcasted_iota(I32, (L, L), 0)
    col = lax.broadcasted_iota(I32, (L, L), 1)
    tril = row >= col
    tril_b = jnp.where(tril, 1.0, 0.0).astype(BF16)
    b = sum(jnp.dot(tril_b, p, preferred_element_type=F32) for p in _split3(logf))
    b_t = b.T
    i_t = g.T
    m_vec = m_s[...]
    lane = lax.broadcasted_iota(I32, (1, HEAD), 1)
    m_out = m_vec

    for h in range(ML_HEADS):
        q = q_ref[:, h * HEAD:(h + 1) * HEAD].astype(BF16)
        kf = k_ref[:, h * HEAD:(h + 1) * HEAD] * (HEAD ** -0.5)
        k = kf.astype(BF16)
        vf = v_ref[:, h * ML_DV:(h + 1) * ML_DV]
        v = vf.astype(BF16)
        c = c_s[h]
        n = n_s[h:h + 1, :]
        m = m_vec[:, h:h + 1]
        b_col = b[:, h:h + 1]
        i_col = g[:, h:h + 1]
        b_row = b_t[h:h + 1, :]
        i_row = i_t[h:h + 1, :]

        d_log = jnp.where(tril, b_col - b_row + i_row, -jnp.inf)
        inter = b_col + m
        m_t = jnp.maximum(jnp.max(d_log, axis=-1, keepdims=True), inter)
        qk = lax.dot_general(q, k, (((1,), (1,)), ((), ())), preferred_element_type=F32)
        w_intra = jnp.exp(d_log - m_t) * qk
        w_inter = jnp.exp(inter - m_t)
        qc = jnp.dot(q, c.astype(BF16), preferred_element_type=F32)
        num = jnp.dot(w_intra.astype(BF16), v, preferred_element_type=F32) + w_inter * qc
        qn = jnp.sum(q.astype(F32) * n, axis=-1, keepdims=True)
        den = jnp.sum(w_intra, axis=-1, keepdims=True) + w_inter * qn
        hh = num / jnp.maximum(jnp.abs(den), jnp.exp(-m_t))

        b_last = b_col[L - 1:L, :]
        log_src = b_last - b_col + i_col
        m_new = jnp.maximum(b_last + m, jnp.max(log_src, axis=0, keepdims=True))
        src_w = jnp.exp(log_src - m_new)
        car_w = jnp.exp(b_last + m - m_new)
        c_s[h] = car_w * c + lax.dot_general(k, (src_w * vf).astype(BF16), (((0,), (0,)), ((), ())),
                                             preferred_element_type=F32)
        n_s[h:h + 1, :] = car_w * n + jnp.sum(src_w * k.astype(F32), axis=0, keepdims=True)
        m_out = jnp.where(lane == h, m_new, m_out)

        hn = hh * lax.rsqrt(jnp.mean(hh * hh, axis=-1, keepdims=True) + EPS)
        gate = jax.nn.sigmoid(o_ref[:, h * ML_DV:(h + 1) * ML_DV])
        y_ref[:, h * ML_DV:(h + 1) * ML_DV] = (hn * nw_ref[:, h * ML_DV:(h + 1) * ML_DV] * gate).astype(y_ref.dtype)

    m_s[...] = m_out


def _mlstm(main, tail, bias, nw, c0, n0, m0, row_block0, batch, nchunks):
    dk = ML_HEADS * HEAD
    dv = ML_HEADS * ML_DV

    def rows(b, c):
        return row_block0 + b * nchunks + c

    return pl.pallas_call(
        _mlstm_kernel,
        grid=(batch, nchunks),
        in_specs=[pl.BlockSpec((GROUP, dk), lambda b, c: (rows(b, c), 6)),
                  pl.BlockSpec((GROUP, dk), lambda b, c: (rows(b, c), 7)),
                  pl.BlockSpec((GROUP, dv), lambda b, c: (rows(b, c), 4)),
                  pl.BlockSpec((GROUP, dv), lambda b, c: (rows(b, c), 5)),
                  pl.BlockSpec((GROUP, HEAD), lambda b, c: (rows(b, c), 64)),
                  pl.BlockSpec((1, HEAD), lambda b, c: (0, 0)),
                  pl.BlockSpec((1, dv), lambda b, c: (0, 0)),
                  pl.BlockSpec((None, ML_HEADS, HEAD, ML_DV), lambda b, c: (b, 0, 0, 0)),
                  pl.BlockSpec((None, ML_HEADS, HEAD), lambda b, c: (b, 0, 0)),
                  pl.BlockSpec((None, 1, HEAD), lambda b, c: (b, 0, 0))],
        out_specs=[pl.BlockSpec((GROUP, dv), lambda b, c: (b * nchunks + c, 0)),
                   pl.BlockSpec((None, ML_HEADS, HEAD, ML_DV), lambda b, c: (b, 0, 0, 0)),
                   pl.BlockSpec((None, ML_HEADS, HEAD), lambda b, c: (b, 0, 0)),
                   pl.BlockSpec((None, 1, HEAD), lambda b, c: (b, 0, 0))],
        out_shape=[jax.ShapeDtypeStruct((batch * nchunks * GROUP, dv), BF16),
                   jax.ShapeDtypeStruct((batch, ML_HEADS, HEAD, ML_DV), F32),
                   jax.ShapeDtypeStruct((batch, ML_HEADS, HEAD), F32),
                   jax.ShapeDtypeStruct((batch, 1, HEAD), F32)],
        compiler_params=_cp("parallel", "arbitrary"),
        name="mlstm",
    )(main, main, main, main, tail, bias, nw, c0, n0, m0)


def _merge_kernel(ysb_ref, yml_ref, w0_ref, w1_ref, gsb_ref, gml_ref, o_ref):
    u_sb = jnp.dot(ysb_ref[...], w0_ref[...].astype(BF16), preferred_element_type=F32)
    u_ml = jnp.dot(yml_ref[...], w1_ref[...].astype(BF16), preferred_element_type=F32)
    o_ref[...] = (jax.nn.sigmoid(gsb_ref[...]) * u_sb + jax.nn.sigmoid(gml_ref[...]) * u_ml).astype(o_ref.dtype)


def _merge(y_sb, y_ml, w_branch, tail):
    t, k = y_sb.shape
    d = w_branch.shape[2]
    tm = _pick(t, (1024, 512, 384, 256, 128))
    tn = _pick(d, (512, 256, 128))
    nj = d // tn
    return pl.pallas_call(
        _merge_kernel,
        grid=(t // tm, nj),
        in_specs=[pl.BlockSpec((tm, k), lambda i, j: (i, 0)),
                  pl.BlockSpec((tm, k), lambda i, j: (i, 0)),
                  pl.BlockSpec((None, k, tn), lambda i, j: (0, 0, j)),
                  pl.BlockSpec((None, k, tn), lambda i, j: (1, 0, j)),
                  pl.BlockSpec((tm, tn), lambda i, j: (i, j)),
                  pl.BlockSpec((tm, tn), lambda i, j: (i, nj + j))],
        out_specs=pl.BlockSpec((tm, tn), lambda i, j: (i, j)),
        out_shape=jax.ShapeDtypeStruct((t, d), BF16),
        compiler_params=_cp("parallel", "parallel"),
        name="merge",
    )(y_sb, y_ml, w_branch, w_branch, tail, tail)


def _oproj_kernel(a_ref, w_ref, x_ref, g_ref, o_ref):
    mix = jnp.dot(a_ref[...], w_ref[...].astype(BF16), preferred_element_type=F32)
    tm, tn = mix.shape
    x3 = x_ref[...].reshape(tm // GROUP, GROUP, tn) + g_ref[...] * mix.reshape(tm // GROUP, GROUP, tn)
    o_ref[...] = x3.reshape(tm, tn)


def _oproj(merged, w_o, x, gate):
    t, k = merged.shape
    d = w_o.shape[1]
    tm = _pick(t, (1024, 512, 384, 256, 128))
    tn = _pick(d, (512, 256, 128))
    return pl.pallas_call(
        _oproj_kernel,
        grid=(t // tm, d // tn),
        in_specs=[pl.BlockSpec((tm, k), lambda i, j: (i, 0)),
                  pl.BlockSpec((k, tn), lambda i, j: (0, j)),
                  pl.BlockSpec((tm, tn), lambda i, j: (i, j)),
                  pl.BlockSpec((tm // GROUP, 1, tn), lambda i, j: (i, 0, j))],
        out_specs=pl.BlockSpec((tm, tn), lambda i, j: (i, j)),
        out_shape=jax.ShapeDtypeStruct((t, d), F32),
        compiler_params=_cp("parallel", "parallel"),
        name="oproj",
    )(merged, w_o, x, gate)


def _pack_halves(h):
    d = h.shape[1]
    lo = pltpu.bitcast(h[:, :d // 2].astype(BF16).astype(F32), U32)
    hi = pltpu.bitcast(h[:, d // 2:].astype(BF16).astype(F32), U32)
    return (hi & jnp.uint32(0xFFFF0000)) | (lo >> 16)


def _unpack_halves(w):
    lo = pltpu.bitcast(w << 16, F32).astype(BF16)
    hi = pltpu.bitcast(w & jnp.uint32(0xFFFF0000), F32).astype(BF16)
    return lo, hi


def _modnorm2_kernel(x_ref, nw_ref, sc_ref, sh_ref, whi_ref, wlo_ref, hp_ref, lg_ref):
    h = _modulated(x_ref[...], nw_ref[...], sc_ref[...], sh_ref[...])
    hp_ref[...] = _pack_halves(h)
    h_hi = h.astype(BF16)
    h_lo = (h - h_hi.astype(F32)).astype(BF16)
    whi = whi_ref[...]
    lg_ref[...] = (jnp.dot(h_hi, whi, preferred_element_type=F32)
                   + jnp.dot(h_hi, wlo_ref[...], preferred_element_type=F32)
                   + jnp.dot(h_lo, whi, preferred_element_type=F32))


def _modnorm2(x, nw, sc, sh, w_router):
    t, d = x.shape
    e = w_router.shape[1]
    tm = _pick(t, (256, 128, 64))
    g = tm // GROUP
    whi = w_router.astype(BF16)
    wlo = (w_router - whi.astype(F32)).astype(BF16)
    return pl.pallas_call(
        _modnorm2_kernel,
        grid=(t // tm,),
        in_specs=[pl.BlockSpec((tm, d), lambda i: (i, 0)),
                  pl.BlockSpec((1, d), lambda i: (0, 0)),
                  pl.BlockSpec((g, 1, d), lambda i: (i, 0, 0)),
                  pl.BlockSpec((g, 1, d), lambda i: (i, 0, 0)),
                  pl.BlockSpec((d, e), lambda i: (0, 0)),
                  pl.BlockSpec((d, e), lambda i: (0, 0))],
        out_specs=[pl.BlockSpec((tm, d // 2), lambda i: (i, 0)),
                   pl.BlockSpec((tm, e), lambda i: (i, 0))],
        out_shape=[jax.ShapeDtypeStruct((t, d // 2), U32),
                   jax.ShapeDtypeStruct((t, e), F32)],
        compiler_params=_cp("parallel"),
        name="modnorm2_router",
    )(x, nw.reshape(1, d), sc, sh, whi, wlo)


def _route_kernel(lg_ref, bias_ref, idx_ref, w_ref):
    s = jax.nn.sigmoid(lg_ref[...])
    tm, e = s.shape
    per = e // N_GROUPS
    sb = s + bias_ref[...]
    lane = lax.broadcasted_iota(I32, (tm, e), 1)
    grp = lane // per
    neg = -jnp.inf

    gscore = []
    for g in range(N_GROUPS):
        vals = jnp.where(grp == g, sb, neg)
        m1 = jnp.max(vals, axis=-1, keepdims=True)
        first = jnp.min(jnp.where(vals == m1, lane, e), axis=-1, keepdims=True)
        m2 = jnp.max(jnp.where(lane == first, neg, vals), axis=-1, keepdims=True)
        gscore.append(m1 + m2)

    emask = jnp.zeros((tm, e), jnp.bool_)
    for g in range(N_GROUPS):
        rank = jnp.zeros((tm, 1), I32)
        for o in range(N_GROUPS):
            if o == g:
                continue
            ahead = (gscore[o] > gscore[g]) | ((gscore[o] == gscore[g]) if o < g else False)
            rank = rank + jnp.where(ahead, 1, 0)
        emask = emask | ((grp == g) & (rank < TOPK_GROUPS))

    vals = jnp.where(emask, sb, neg)
    out_lane = lax.broadcasted_iota(I32, (tm, TOP_K), 1)
    idx_out = jnp.zeros((tm, TOP_K), I32)
    w_out = jnp.zeros((tm, TOP_K), F32)
    for kk in range(TOP_K):
        mx = jnp.max(vals, axis=-1, keepdims=True)
        first = jnp.min(jnp.where(vals == mx, lane, e), axis=-1, keepdims=True)
        hit = lane == first
        wk = jnp.sum(jnp.where(hit, s, 0.0), axis=-1, keepdims=True)
        vals = jnp.where(hit, neg, vals)
        idx_out = jnp.where(out_lane == kk, first, idx_out)
        w_out = jnp.where(out_lane == kk, wk, w_out)
    idx_ref[...] = idx_out
    w_ref[...] = w_out / jnp.sum(w_out, axis=-1, keepdims=True) * ROUTED_SCALE


def _route(logits, router_bias):
    t, e = logits.shape
    tm = _pick(t, (256, 128, 64))
    return pl.pallas_call(
        _route_kernel,
        grid=(t // tm,),
        in_specs=[pl.BlockSpec((tm, e), lambda i: (i, 0)),
                  pl.BlockSpec((1, e), lambda i: (0, 0))],
        out_specs=[pl.BlockSpec((tm, TOP_K), lambda i: (i, 0)),
                   pl.BlockSpec((tm, TOP_K), lambda i: (i, 0))],
        out_shape=[jax.ShapeDtypeStruct((t, TOP_K), I32),
                   jax.ShapeDtypeStruct((t, TOP_K), F32)],
        compiler_params=_cp("parallel"),
        name="route",
    )(logits, router_bias.reshape(1, e))


def _dispatch_kernel(dest_ref, x_hbm, o_hbm, sem):
    i = pl.program_id(0)
    tt = dest_ref.shape[2] // TOP_K

    def copy(t, kk):
        return pltpu.make_async_copy(x_hbm.at[pl.ds(i * tt + t, 1)],
                                     o_hbm.at[pl.ds(dest_ref[0, 0, t * TOP_K + kk], 1)], sem)

    def start(t, _):
        for kk in range(TOP_K):
            copy(t, kk).start()
        return 0

    def wait(t, _):
        for kk in range(TOP_K):
            copy(t, kk).wait()
        return 0

    lax.fori_loop(0, tt, start, 0)
    lax.fori_loop(0, tt, wait, 0)


def _dispatch(hp, dest):
    t, w = hp.shape
    tt = _pick(t, (128, 64))
    return pl.pallas_call(
        _dispatch_kernel,
        grid=(t // tt,),
        in_specs=[pl.BlockSpec((1, 1, tt * TOP_K), lambda i: (i, 0, 0), memory_space=pltpu.SMEM),
                  pl.BlockSpec(memory_space=pl.ANY)],
        out_specs=pl.BlockSpec(memory_space=pl.ANY),
        out_shape=jax.ShapeDtypeStruct((t * TOP_K, w), U32),
        scratch_shapes=[pltpu.SemaphoreType.DMA(())],
        compiler_params=pltpu.CompilerParams(dimension_semantics=("arbitrary",), has_side_effects=True),
        name="dispatch",
    )(dest.reshape(t // tt, 1, tt * TOP_K), hp)


def _pair_rows(lo_ref, hi_ref, shape):
    i = pl.program_id(0)
    rows = lax.broadcasted_iota(I32, shape, 0)
    return (rows >= lo_ref[i]) & (rows < hi_ref[i])


def _expert_up_kernel(pe_ref, pb_ref, lo_ref, hi_ref, fe_ref, fb_ref, x_ref, wg_ref, wu_ref, h_ref, wg_s, wu_s):
    i = pl.program_id(0)

    @pl.when(fe_ref[i] == 1)
    def _():
        wg_s[...] = wg_ref[...].astype(BF16)
        wu_s[...] = wu_ref[...].astype(BF16)

    @pl.when(hi_ref[i] > lo_ref[i])
    def _():
        lo, hi = _unpack_halves(x_ref[...])
        half = lo.shape[1]
        gt = (jnp.dot(lo, wg_s[:half, :], preferred_element_type=F32)
              + jnp.dot(hi, wg_s[half:, :], preferred_element_type=F32))
        up = (jnp.dot(lo, wu_s[:half, :], preferred_element_type=F32)
              + jnp.dot(hi, wu_s[half:, :], preferred_element_type=F32))
        hv = (gt * jax.nn.sigmoid(gt) * up).astype(h_ref.dtype)
        mine = _pair_rows(lo_ref, hi_ref, hv.shape)

        @pl.when(fb_ref[i] == 1)
        def _():
            h_ref[...] = jnp.where(mine, hv, jnp.zeros_like(hv))

        @pl.when(fb_ref[i] == 0)
        def _():
            h_ref[...] = jnp.where(mine, hv, h_ref[...])


def _expert_down_kernel(pe_ref, pb_ref, lo_ref, hi_ref, fe_ref, fb_ref, h_ref, wd_ref, o_ref, wd_s):
    i = pl.program_id(0)

    @pl.when(fe_ref[i] == 1)
    def _():
        wd_s[...] = wd_ref[...].astype(BF16)

    @pl.when(hi_ref[i] > lo_ref[i])
    def _():
        ov = jnp.dot(h_ref[...], wd_s[...], preferred_element_type=F32)
        mine = _pair_rows(lo_ref, hi_ref, ov.shape)

        @pl.when(fb_ref[i] == 1)
        def _():
            o_ref[...] = jnp.where(mine, ov, 0.0)

        @pl.when(fb_ref[i] == 0)
        def _():
            o_ref[...] = jnp.where(mine, ov, o_ref[...])


def _experts(xs, pairs, w_gate, w_up, w_down):
    a, half = xs.shape
    ne, d, de = w_gate.shape
    npairs = pairs[0].shape[0]

    def rows(i, pe, pb, lo, hi, fe, fb):
        return (pb[i], 0)

    def wsel(i, pe, pb, lo, hi, fe, fb):
        return (pe[i], 0, 0)

    h = pl.pallas_call(
        _expert_up_kernel,
        grid_spec=pltpu.PrefetchScalarGridSpec(
            num_scalar_prefetch=len(pairs),
            grid=(npairs,),
            in_specs=[pl.BlockSpec((MOE_BLOCK, half), rows),
                      pl.BlockSpec((None, d, de), wsel),
                      pl.BlockSpec((None, d, de), wsel)],
            out_specs=pl.BlockSpec((MOE_BLOCK, de), rows),
            scratch_shapes=[pltpu.VMEM((d, de), BF16), pltpu.VMEM((d, de), BF16)]),
        out_shape=jax.ShapeDtypeStruct((a, de), BF16),
        compiler_params=_cp("arbitrary"),
        name="expert_up",
    )(*pairs, xs, w_gate, w_up)

    return pl.pallas_call(
        _expert_down_kernel,
        grid_spec=pltpu.PrefetchScalarGridSpec(
            num_scalar_prefetch=len(pairs),
            grid=(npairs,),
            in_specs=[pl.BlockSpec((MOE_BLOCK, de), rows),
                      pl.BlockSpec((None, de, d), wsel)],
            out_specs=pl.BlockSpec((MOE_BLOCK, d), rows),
            scratch_shapes=[pltpu.VMEM((de, d), BF16)]),
        out_shape=jax.ShapeDtypeStruct((a, d), F32),
        compiler_params=_cp("arbitrary"),
        name="expert_down",
    )(*pairs, h, w_down)


def _shared_kernel(x_ref, wg_ref, wu_ref, wd_ref, o_ref):
    lo, hi = _unpack_halves(x_ref[...])
    half = lo.shape[1]
    gt = (jnp.dot(lo, wg_ref[:half, :], preferred_element_type=F32)
          + jnp.dot(hi, wg_ref[half:, :], preferred_element_type=F32))
    up = (jnp.dot(lo, wu_ref[:half, :], preferred_element_type=F32)
          + jnp.dot(hi, wu_ref[half:, :], preferred_element_type=F32))
    hh = (gt * jax.nn.sigmoid(gt) * up).astype(BF16)
    o_ref[...] = jnp.dot(hh, wd_ref[...], preferred_element_type=F32)


def _shared(hp, w_gate, w_up, w_down):
    t, half = hp.shape
    d, de = w_gate.shape
    tm = _pick(t, (256, 128, 64))
    return pl.pallas_call(
        _shared_kernel,
        grid=(t // tm,),
        in_specs=[pl.BlockSpec((tm, half), lambda i: (i, 0)),
                  pl.BlockSpec((d, de), lambda i: (0, 0)),
                  pl.BlockSpec((d, de), lambda i: (0, 0)),
                  pl.BlockSpec((de, d), lambda i: (0, 0))],
        out_specs=pl.BlockSpec((tm, d), lambda i: (i, 0)),
        out_shape=jax.ShapeDtypeStruct((t, d), F32),
        compiler_params=_cp("parallel"),
        name="shared_expert",
    )(hp, w_gate.astype(BF16), w_up.astype(BF16), w_down.astype(BF16))


def _combine_kernel(dest_ref, w_ref, sh_ref, x_ref, g_ref, nw_ref, y_hbm, o_ref, buf, sem):
    tt = x_ref.shape[0]

    def copy(t, kk):
        return pltpu.make_async_copy(y_hbm.at[pl.ds(dest_ref[0, 0, t * TOP_K + kk], 1)],
                                     buf.at[kk, pl.ds(t, 1)], sem)

    def start(t, _):
        for kk in range(TOP_K):
            copy(t, kk).start()
        return 0

    def wait(t, _):
        for kk in range(TOP_K):
            copy(t, kk).wait()
        return 0

    lax.fori_loop(0, tt, start, 0)
    lax.fori_loop(0, tt, wait, 0)

    w = w_ref[...]
    ff = sh_ref[...]
    for kk in range(TOP_K):
        ff = ff + w[:, kk:kk + 1] * buf[kk]
    x = x_ref[...] + g_ref[0] * ff
    o_ref[...] = x * lax.rsqrt(jnp.mean(x * x, axis=-1, keepdims=True) + EPS) * nw_ref[...]


def _combine(y_sorted, dest, gw, shared, x, gate, nw):
    t, d = x.shape
    tt = GROUP
    return pl.pallas_call(
        _combine_kernel,
        grid=(t // tt,),
        in_specs=[pl.BlockSpec((1, 1, tt * TOP_K), lambda i: (i, 0, 0), memory_space=pltpu.SMEM),
                  pl.BlockSpec((tt, TOP_K), lambda i: (i, 0)),
                  pl.BlockSpec((tt, d), lambda i: (i, 0)),
                  pl.BlockSpec((tt, d), lambda i: (i, 0)),
                  pl.BlockSpec((1, 1, d), lambda i: (i, 0, 0)),
                  pl.BlockSpec((1, d), lambda i: (0, 0)),
                  pl.BlockSpec(memory_space=pl.ANY)],
        out_specs=pl.BlockSpec((tt, d), lambda i: (i, 0)),
        out_shape=jax.ShapeDtypeStruct((t, d), F32),
        scratch_shapes=[pltpu.VMEM((TOP_K, tt, d), F32), pltpu.SemaphoreType.DMA(())],
        compiler_params=_cp("arbitrary"),
        name="combine_final",
    )(dest.reshape(t // tt, 1, tt * TOP_K), gw, shared, x, gate, nw.reshape(1, d), y_sorted)


def _dispatch_plan(eidx, n_experts):
    t = eidx.shape[0]
    a = t * TOP_K
    assert a % MOE_BLOCK == 0
    flat_e = eidx.reshape(-1)
    order = jnp.argsort(flat_e)
    dest = jnp.zeros((a,), I32).at[order].set(jnp.arange(a, dtype=I32))
    counts = jnp.zeros((n_experts,), I32).at[flat_e].add(1)
    cend = jnp.cumsum(counts)
    cstart = cend - counts
    first_blk = cstart // MOE_BLOCK
    nblk = jnp.where(counts > 0, (cend - 1) // MOE_BLOCK - first_blk + 1, 0)
    pend = jnp.cumsum(nblk)
    pstart = pend - nblk
    npairs = a // MOE_BLOCK + n_experts
    p = jnp.arange(npairs, dtype=I32)
    used = p < pend[-1]
    pc = jnp.minimum(p, pend[-1] - 1)
    pe = jnp.minimum(jnp.searchsorted(pend, pc, side='right'), n_experts - 1).astype(I32)
    pb = first_blk[pe] + pc - pstart[pe]
    lo = jnp.maximum(cstart[pe] - pb * MOE_BLOCK, 0)
    hi = jnp.where(used, jnp.minimum(cend[pe] - pb * MOE_BLOCK, MOE_BLOCK), lo)
    prev = lambda v: jnp.concatenate([jnp.full((1,), -1, I32), v[:-1]])
    fe = jnp.where(used & (pe != prev(pe)), 1, 0)
    fb = jnp.where(used & (pb != prev(pb)), 1, 0)
    return dest, tuple(v.astype(I32) for v in (pe, pb, lo, hi, fe, fb))


def kernel(x_prompt, x_sample, cache_sb_k, cache_sb_v, state_mlstm_c, state_mlstm_n, state_mlstm_m, c_prompt, c_sample, w_ada, b_ada, norm1_w, w_in, b_ig, b_fg, ml_norm_w, w_branch, w_o, norm2_w, w_router, router_bias, w_e_gate, w_e_up, w_e_down, w_s_gate, w_s_up, w_s_down, norm_f_w):
    bp, sp, d = x_prompt.shape
    bs, ss, _ = x_sample.shape
    depth = w_ada.shape[0]
    assert depth == 1 and ss == GROUP and sp % SB_TQ == 0
    tp, ts = bp * sp, bs * ss
    t = tp + ts
    d_sb = SB_HEADS * HEAD
    d_mlk = ML_HEADS * HEAD
    d_ml = ML_HEADS * ML_DV
    n_main = 3 * d_sb + 2 * d_mlk + 2 * d_ml
    n_experts = w_router.shape[2]
    l = 0

    x = jnp.concatenate([x_prompt.reshape(tp, d), x_sample.reshape(ts, d)], axis=0)

    c_all = jnp.concatenate([c_prompt, c_sample, jnp.zeros((-(bp + bs) % 8, d), F32)], axis=0)
    mod = _ada(c_all, w_ada[l], b_ada[l])
    mod_g = jnp.concatenate([jnp.repeat(mod[:bp], sp // GROUP, axis=0), mod[bp:bp + bs]], axis=0)
    sh1, sc1, g1, sh2, sc2, g2 = [m.reshape(t // GROUP, 1, d) for m in jnp.split(mod_g, 6, axis=-1)]

    hm = _modnorm(x, norm1_w[l], sc1, sh1)
    main = _matmul(hm, w_in[l], n_main, "in_proj_main")
    w_tail = jnp.concatenate([w_in[l][:, n_main + 2 * ML_HEADS:], w_in[l][:, n_main:n_main + 2 * ML_HEADS],
                              jnp.zeros((d, HEAD - 2 * ML_HEADS), F32)], axis=1).astype(BF16)
    tail = _matmul(hm, w_tail, w_tail.shape[1], "in_proj_tail")

    y_sb_p = _sb_prompt(main, bp, sp)
    y_sb_s = _sb_sample(main, cache_sb_k[l].reshape(bs, -1, d_sb), cache_sb_v[l].reshape(bs, -1, d_sb),
                        tp // GROUP, bs, ss)
    y_sb = jnp.concatenate([y_sb_p, y_sb_s], axis=0)

    gate_bias = jnp.concatenate([b_ig[l], b_fg[l], jnp.zeros((HEAD - 2 * ML_HEADS,), F32)]).reshape(1, HEAD)
    nw_ml = ml_norm_w[l].reshape(1, d_ml)
    y_ml_p, c_p, n_p, m_p = _mlstm(main, tail, gate_bias, nw_ml,
                                   jnp.zeros((bp, ML_HEADS, HEAD, ML_DV), F32),
                                   jnp.zeros((bp, ML_HEADS, HEAD), F32),
                                   jnp.zeros((bp, 1, HEAD), F32), 0, bp, sp // GROUP)
    m0_s = jnp.pad(state_mlstm_m[l], ((0, 0), (0, HEAD - ML_HEADS))).reshape(bs, 1, HEAD)
    y_ml_s, c_s, n_s, m_s = _mlstm(main, tail, gate_bias, nw_ml, state_mlstm_c[l], state_mlstm_n[l], m0_s,
                                   tp // GROUP, bs, 1)
    y_ml = jnp.concatenate([y_ml_p, y_ml_s], axis=0)

    merged = _merge(y_sb, y_ml, w_branch[l], tail)
    x1 = _oproj(merged, w_o[l], x, g1)

    hp, logits = _modnorm2(x1, norm2_w[l], sc2, sh2, w_router[l])
    eidx, gw = _route(logits, router_bias[l])
    dest, pairs = _dispatch_plan(eidx, n_experts)
    xs = _dispatch(hp, dest)
    y_sorted = _experts(xs, pairs, w_e_gate[l], w_e_up[l], w_e_down[l])
    shared = _shared(hp, w_s_gate[l], w_s_up[l], w_s_down[l])
    y = _combine(y_sorted, dest, gw, shared, x1, g2, norm_f_w)

    k_sb = main[:, d_sb:2 * d_sb]
    v_sb = main[:, 2 * d_sb:3 * d_sb]
    return (y[:tp].reshape(bp, sp, d), y[tp:].reshape(bs, ss, d),
            k_sb[:tp].reshape(1, bp, sp, SB_HEADS, HEAD), v_sb[:tp].reshape(1, bp, sp, SB_HEADS, HEAD),
            c_p[None], n_p[None], m_p[:, 0, :ML_HEADS][None],
            k_sb[tp:].reshape(1, bs, ss, SB_HEADS, HEAD), v_sb[tp:].reshape(1, bs, ss, SB_HEADS, HEAD),
            c_s[None], n_s[None], m_s[:, 0, :ML_HEADS][None])
```

```python
import jax
import jax.numpy as jnp
from jax import lax
from jax.experimental import pallas as pl
from jax.experimental.pallas import tpu as pltpu

F32 = jnp.float32
BF16 = jnp.bfloat16
U32 = jnp.uint32
I32 = jnp.int32

EPS = 1e-6
GROUP = 64
HEAD = 128
SB_HEADS = 16
ML_HEADS = 8
ML_DV = 256
N_GROUPS = 8
TOPK_GROUPS = 4
TOP_K = 8
ROUTED_SCALE = 2.5
MOE_BLOCK = 128
SB_TQ = 512
SB_TK = 128
SB_TKB = 512
SB_CACHE_BLOCK = SB_TKB
SB_SAMPLE_UNROLL = 8
GATHER_ROWS = 512
VMEM_LIMIT = 56 * 1024 * 1024


def _cp(*sem):
    return pltpu.CompilerParams(dimension_semantics=sem, vmem_limit_bytes=VMEM_LIMIT)


def _pick(n, prefs):
    for p in prefs:
        if n % p == 0:
            return p
    return n


def _ada_kernel(c_ref, w_ref, b_ref, o_ref):
    c = c_ref[...]
    a = (c * jax.nn.sigmoid(c)).astype(BF16)
    o_ref[...] = jnp.dot(a, w_ref[...].astype(BF16), preferred_element_type=F32) + b_ref[...]


def _ada(c, w, b):
    m, k = c.shape
    n = w.shape[1]
    tn = _pick(n, (512, 256, 128))
    return pl.pallas_call(
        _ada_kernel,
        grid=(n // tn,),
        in_specs=[pl.BlockSpec((m, k), lambda j: (0, 0)),
                  pl.BlockSpec((k, tn), lambda j: (0, j)),
                  pl.BlockSpec((1, tn), lambda j: (0, j))],
        out_specs=pl.BlockSpec((m, tn), lambda j: (0, j)),
        out_shape=jax.ShapeDtypeStruct((m, n), F32),
        compiler_params=_cp("parallel"),
        name="ada",
    )(c, w, b.reshape(1, n))


def _modulated(x, nw, sc, sh):
    tm, d = x.shape
    ms = jnp.mean(x * x, axis=-1, keepdims=True)
    y = x * lax.rsqrt(ms + EPS) * nw
    y3 = y.reshape(tm // GROUP, GROUP, d) * (1.0 + sc) + sh
    return y3.reshape(tm, d)


def _modnorm_kernel(x_ref, nw_ref, sc_ref, sh_ref, o_ref):
    o_ref[...] = _modulated(x_ref[...], nw_ref[...], sc_ref[...], sh_ref[...]).astype(o_ref.dtype)


def _modnorm(x, nw, sc, sh):
    t, d = x.shape
    tm = _pick(t, (256, 128, 64))
    g = tm // GROUP
    return pl.pallas_call(
        _modnorm_kernel,
        grid=(t // tm,),
        in_specs=[pl.BlockSpec((tm, d), lambda i: (i, 0)),
                  pl.BlockSpec((1, d), lambda i: (0, 0)),
                  pl.BlockSpec((g, 1, d), lambda i: (i, 0, 0)),
                  pl.BlockSpec((g, 1, d), lambda i: (i, 0, 0))],
        out_specs=pl.BlockSpec((tm, d), lambda i: (i, 0)),
        out_shape=jax.ShapeDtypeStruct((t, d), BF16),
        compiler_params=_cp("parallel"),
        name="modnorm1",
    )(x, nw.reshape(1, d), sc, sh)


def _mm_kernel(a_ref, w_ref, o_ref):
    o_ref[...] = jnp.dot(a_ref[...], w_ref[...].astype(BF16),
                         preferred_element_type=F32).astype(o_ref.dtype)


def _matmul(a, w, n, name):
    m, k = a.shape
    tm = _pick(m, (1024, 512, 384, 256, 128))
    tn = _pick(n, (512, 640, 256, 128))
    return pl.pallas_call(
        _mm_kernel,
        grid=(m // tm, n // tn),
        in_specs=[pl.BlockSpec((tm, k), lambda i, j: (i, 0)),
                  pl.BlockSpec((k, tn), lambda i, j: (0, j))],
        out_specs=pl.BlockSpec((tm, tn), lambda i, j: (i, j)),
        out_shape=jax.ShapeDtypeStruct((m, n), F32),
        compiler_params=_cp("parallel", "parallel"),
        name=name,
    )(a, w)


IN_TN = 512
IN_HEADS = IN_TN // HEAD
IN_SB_TILES = SB_HEADS // IN_HEADS


def _in_proj_kernel(a_ref, w_ref, qkv_ref, k_ref, v_ref, ml_ref):
    j = pl.program_id(1)
    res = jnp.dot(a_ref[...], w_ref[...], preferred_element_type=F32)

    @pl.when(j < 3 * IN_SB_TILES)
    def _():
        qkv_ref[...] = res.astype(qkv_ref.dtype)

    @pl.when(j >= 3 * IN_SB_TILES)
    def _():
        ml_ref[...] = res

    for jj in range(IN_SB_TILES):
        @pl.when(j == IN_SB_TILES + jj)
        def _():
            for hh in range(IN_HEADS):
                k_ref[:, jj * IN_HEADS + hh, :] = res[:, hh * HEAD:(hh + 1) * HEAD]

        @pl.when(j == 2 * IN_SB_TILES + jj)
        def _():
            for hh in range(IN_HEADS):
                v_ref[:, jj * IN_HEADS + hh, :] = res[:, hh * HEAD:(hh + 1) * HEAD]


def _in_proj(a, w, n_sb, n_ml):
    m, k = a.shape
    tm = _pick(m, (512, 384, 256, 128))
    tn = IN_TN
    nsb = n_sb // tn
    return pl.pallas_call(
        _in_proj_kernel,
        grid=(m // tm, (n_sb + n_ml) // tn),
        in_specs=[pl.BlockSpec((tm, k), lambda i, j: (i, 0)),
                  pl.BlockSpec((k, tn), lambda i, j: (0, j))],
        out_specs=[pl.BlockSpec((tm, tn), lambda i, j: (i, jnp.minimum(j, nsb - 1))),
                   pl.BlockSpec((tm, SB_HEADS, HEAD), lambda i, j: (i, 0, 0)),
                   pl.BlockSpec((tm, SB_HEADS, HEAD), lambda i, j: (i, 0, 0)),
                   pl.BlockSpec((tm, tn), lambda i, j: (i, jnp.maximum(j - nsb, 0)))],
        out_shape=[jax.ShapeDtypeStruct((m, n_sb), BF16),
                   jax.ShapeDtypeStruct((m, SB_HEADS, HEAD), F32),
                   jax.ShapeDtypeStruct((m, SB_HEADS, HEAD), F32),
                   jax.ShapeDtypeStruct((m, n_ml), F32)],
        compiler_params=_cp("parallel", "arbitrary"),
        name="in_proj_main",
    )(a, w)


def _softplus(z):
    return jnp.maximum(z, 0.0) + jnp.log(1.0 + jnp.exp(-jnp.abs(z)))


def _sb_step(q, k, v, carry, acc, tri, mask):
    z = lax.dot_general(q, k, (((1,), (1,)), ((), ())), preferred_element_type=F32) * (HEAD ** -0.5)
    sp = _softplus(z)
    spm = sp if mask is None else jnp.where(mask, sp, 0.0)
    hi = spm.astype(BF16)
    mid = (spm - hi.astype(F32)).astype(BF16)
    excl = (jnp.dot(hi, tri, preferred_element_type=F32)
            + jnp.dot(mid, tri, preferred_element_type=F32) + carry)
    a = jnp.exp(z - sp - excl)
    if mask is not None:
        a = jnp.where(mask, a, 0.0)
    acc = acc + jnp.dot(a.astype(BF16), v, preferred_element_type=F32)
    carry = carry + jnp.sum(spm, axis=1, keepdims=True)
    return carry, acc


def _tri(n):
    r = lax.broadcasted_iota(I32, (n, n), 0)
    c = lax.broadcasted_iota(I32, (n, n), 1)
    return jnp.where(r > c, 1.0, 0.0).astype(BF16)


def _tri2():
    t = _tri(SB_TK)
    return jnp.concatenate([t, t], axis=0)


def _sb_wide_step(q, k, v, carry, acc, tri2, mask):
    if not isinstance(q, (list, tuple)):
        q, k, v = [q], [k], [v]
    rows = q[0].shape[0]
    tq = rows * len(q)
    nsub = k[0].shape[0] // SB_TK
    z = jnp.concatenate([lax.dot_general(qh, kh, (((1,), (1,)), ((), ())), preferred_element_type=F32)
                         for qh, kh in zip(q, k)], axis=0) * (HEAD ** -0.5)
    sp = _softplus(z)
    spm = sp if mask is None else jnp.where(mask, sp, 0.0)
    hi = spm.astype(BF16)
    mid = (spm - hi.astype(F32)).astype(BF16)
    cols = [slice(j * SB_TK, (j + 1) * SB_TK) for j in range(nsub)]
    lhs = jnp.concatenate([jnp.concatenate([hi[:, c], mid[:, c]], axis=1) for c in cols], axis=0)
    within = jnp.dot(lhs, tri2, preferred_element_type=F32)
    excl = [None] * nsub
    off = carry
    for j in range(nsub - 1, -1, -1):
        excl[j] = within[j * tq:(j + 1) * tq] + off
        off = off + jnp.sum(spm[:, cols[j]], axis=1, keepdims=True)
    a = jnp.exp(z - sp - jnp.concatenate(excl, axis=1))
    if mask is not None:
        a = jnp.where(mask, a, 0.0)
    ab = a.astype(BF16)
    acc = acc + jnp.concatenate([jnp.dot(ab[g * rows:(g + 1) * rows], vh, preferred_element_type=F32)
                                 for g, vh in enumerate(v)], axis=0)
    return off, acc


def _sb_prompt_kernel(q_ref, k_ref, v_ref, o_ref):
    qi = pl.program_id(2)
    tq, tkb = SB_TQ, SB_TKB
    q = q_ref[...]
    tri2 = _tri2()

    def load(kb):
        rows = pl.ds(pl.multiple_of(kb * tkb, tkb), tkb)
        return k_ref[rows, :], v_ref[rows, :]

    k, v = load(qi)
    causal = lax.broadcasted_iota(I32, (tq, tkb), 1) < lax.broadcasted_iota(I32, (tq, tkb), 0)
    carry, acc = _sb_wide_step(q, k, v, jnp.zeros((tq, 1), F32), jnp.zeros((tq, HEAD), F32), tri2, causal)

    def body(i, ca):
        k, v = load(qi - 1 - i)
        return _sb_wide_step(q, k, v, ca[0], ca[1], tri2, None)

    carry, acc = lax.fori_loop(0, qi, body, (carry, acc))
    o_ref[...] = acc.astype(o_ref.dtype)


def _sb_prompt(qkv, batch, seq):
    assert SB_TQ == SB_TKB and seq % SB_TQ == 0
    nq = seq // SB_TQ
    return pl.pallas_call(
        _sb_prompt_kernel,
        grid=(batch, SB_HEADS, nq),
        in_specs=[pl.BlockSpec((SB_TQ, HEAD), lambda b, h, i: (b * nq + i, h)),
                  pl.BlockSpec((seq, HEAD), lambda b, h, i: (b, SB_HEADS + h)),
                  pl.BlockSpec((seq, HEAD), lambda b, h, i: (b, 2 * SB_HEADS + h))],
        out_specs=pl.BlockSpec((SB_TQ, HEAD), lambda b, h, i: (b * nq + i, h)),
        out_shape=jax.ShapeDtypeStruct((batch * seq, SB_HEADS * HEAD), BF16),
        compiler_params=_cp("parallel", "parallel", "arbitrary"),
        name="sb_prompt",
    )(qkv, qkv, qkv)


def _sb_sample_kernel(q_ref, kn_ref, vn_ref, kc_ref, vc_ref, o_ref, carry_ref, acc_ref):
    s = pl.program_id(1)
    ns = pl.num_programs(1)
    lq = q_ref.shape[0]

    @pl.when(s == 0)
    def _():
        tri = _tri(lq)
        mask = (lax.broadcasted_iota(I32, (lq, lq), 1) < lax.broadcasted_iota(I32, (lq, lq), 0))

        def head(h, _):
            cols = pl.ds(pl.multiple_of(h * HEAD, HEAD), HEAD)
            rows = pl.ds(pl.multiple_of(h * lq, lq), lq)
            carry, acc = _sb_step(q_ref[:, cols], kn_ref[:, cols], vn_ref[:, cols],
                                  jnp.zeros((lq, 1), F32), jnp.zeros((lq, HEAD), F32), tri, mask)
            carry_ref[rows, :] = carry
            acc_ref[rows, :] = acc
            return 0

        lax.fori_loop(0, SB_HEADS, head, 0, unroll=2)

    @pl.when(s > 0)
    def _():
        tri2 = _tri2()
        g = SB_SAMPLE_UNROLL

        def heads(i, _):
            cols = [pl.ds(pl.multiple_of((i * g + u) * HEAD, HEAD), HEAD) for u in range(g)]
            rows = pl.ds(pl.multiple_of(i * (g * lq), g * lq), g * lq)
            carry, acc = _sb_wide_step([q_ref[:, c] for c in cols],
                                       [kc_ref[:, c].astype(BF16) for c in cols],
                                       [vc_ref[:, c].astype(BF16) for c in cols],
                                       carry_ref[rows, :], acc_ref[rows, :], tri2, None)
            carry_ref[rows, :] = carry
            acc_ref[rows, :] = acc
            return 0

        lax.fori_loop(0, SB_HEADS // g, heads, 0)

    @pl.when(s == ns - 1)
    def _():
        for h in range(SB_HEADS):
            o_ref[:, h * HEAD:(h + 1) * HEAD] = acc_ref[h * lq:(h + 1) * lq, :].astype(o_ref.dtype)


def _sb_sample(main, cache_k, cache_v, row_block0, batch, lq):
    past = cache_k.shape[1]
    cb = SB_CACHE_BLOCK
    assert past % cb == 0
    ncb = past // cb
    dsb = SB_HEADS * HEAD

    def cache_map(b, s):
        return (b, jnp.maximum(ncb - s, 0) - jnp.where(s == 0, 1, 0), 0)

    return pl.pallas_call(
        _sb_sample_kernel,
        grid=(batch, ncb + 1),
        in_specs=[pl.BlockSpec((lq, dsb), lambda b, s: (row_block0 + b, 0)),
                  pl.BlockSpec((lq, dsb), lambda b, s: (row_block0 + b, 1)),
                  pl.BlockSpec((lq, dsb), lambda b, s: (row_block0 + b, 2)),
                  pl.BlockSpec((None, cb, dsb), cache_map),
                  pl.BlockSpec((None, cb, dsb), cache_map)],
        out_specs=pl.BlockSpec((lq, dsb), lambda b, s: (b, 0)),
        out_shape=jax.ShapeDtypeStruct((batch * lq, dsb), BF16),
        scratch_shapes=[pltpu.VMEM((SB_HEADS * lq, 1), F32),
                        pltpu.VMEM((SB_HEADS * lq, HEAD), F32)],
        compiler_params=_cp("parallel", "arbitrary"),
        name="sb_sample",
    )(main, main, main, cache_k, cache_v)


def _split3(x):
    p0 = x.astype(BF16)
    r = x - p0.astype(F32)
    p1 = r.astype(BF16)
    p2 = (r - p1.astype(F32)).astype(BF16)
    return p0, p1, p2


def _mlstm_kernel(q_ref, k_ref, v_ref, o_ref, if_ref, bias_ref, nw_ref, c0_ref, n0_ref, m0_ref,
                  y_ref, c_s, n_s, m_s):
    ci = pl.program_id(1)
    L = GROUP

    @pl.when(ci == 0)
    def _():
        c_s[...] = c0_ref[...]
        n_s[...] = n0_ref[...]
        m_s[...] = m0_ref[...]

    g = if_ref[...] + bias_ref[...]
    logf = pltpu.roll(jax.nn.log_sigmoid(g), HEAD - ML_HEADS, axis=1)
    row = lax.broadcasted_iota(I32, (L, L), 0)
    col = lax.broadcasted_iota(I32, (L, L), 1)
    tril = row >= col
    tril_b = jnp.where(tril, 1.0, 0.0).astype(BF16)
    b = sum(jnp.dot(tril_b, p, preferred_element_type=F32) for p in _split3(logf))
    b_t = b.T
    i_t = g.T
    m_vec = m_s[...]
    lane = lax.broadcasted_iota(I32, (1, HEAD), 1)
    m_out = m_vec

    for h in range(ML_HEADS):
        q = q_ref[:, h * HEAD:(h + 1) * HEAD].astype(BF16)
        kf = k_ref[:, h * HEAD:(h + 1) * HEAD] * (HEAD ** -0.5)
        k = kf.astype(BF16)
        vf = v_ref[:, h * ML_DV:(h + 1) * ML_DV]
        v = vf.astype(BF16)
        c = c_s[h]
        n = n_s[h:h + 1, :]
        m = m_vec[:, h:h + 1]
        b_col = b[:, h:h + 1]
        i_col = g[:, h:h + 1]
        b_row = b_t[h:h + 1, :]
        i_row = i_t[h:h + 1, :]

        d_log = jnp.where(tril, b_col - b_row + i_row, -jnp.inf)
        inter = b_col + m
        m_t = jnp.maximum(jnp.max(d_log, axis=-1, keepdims=True), inter)
        qk = lax.dot_general(q, k, (((1,), (1,)), ((), ())), preferred_element_type=F32)
        w_intra = jnp.exp(d_log - m_t) * qk
        w_inter = jnp.exp(inter - m_t)
        qc = jnp.dot(q, c.astype(BF16), preferred_element_type=F32)
        num = jnp.dot(w_intra.astype(BF16), v, preferred_element_type=F32) + w_inter * qc
        qn = jnp.sum(q.astype(F32) * n, axis=-1, keepdims=True)
        den = jnp.sum(w_intra, axis=-1, keepdims=True) + w_inter * qn
        hh = num / jnp.maximum(jnp.abs(den), jnp.exp(-m_t))

        b_last = b_col[L - 1:L, :]
        log_src = b_last - b_col + i_col
        m_new = jnp.maximum(b_last + m, jnp.max(log_src, axis=0, keepdims=True))
        src_w = jnp.exp(log_src - m_new)
        car_w = jnp.exp(b_last + m - m_new)
        c_s[h] = car_w * c + lax.dot_general(k, (src_w * vf).astype(BF16), (((0,), (0,)), ((), ())),
                                             preferred_element_type=F32)
        n_s[h:h + 1, :] = car_w * n + jnp.sum(src_w * k.astype(F32), axis=0, keepdims=True)
        m_out = jnp.where(lane == h, m_new, m_out)

        hn = hh * lax.rsqrt(jnp.mean(hh * hh, axis=-1, keepdims=True) + EPS)
        gate = jax.nn.sigmoid(o_ref[:, h * ML_DV:(h + 1) * ML_DV])
        y_ref[:, h * ML_DV:(h + 1) * ML_DV] = (hn * nw_ref[:, h * ML_DV:(h + 1) * ML_DV] * gate).astype(y_ref.dtype)

    m_s[...] = m_out


def _mlstm(main, tail, bias, nw, c0, n0, m0, row_block0, batch, nchunks):
    dk = ML_HEADS * HEAD
    dv = ML_HEADS * ML_DV

    def rows(b, c):
        return row_block0 + b * nchunks + c

    return pl.pallas_call(
        _mlstm_kernel,
        grid=(batch, nchunks),
        in_specs=[pl.BlockSpec((GROUP, dk), lambda b, c: (rows(b, c), 0)),
                  pl.BlockSpec((GROUP, dk), lambda b, c: (rows(b, c), 1)),
                  pl.BlockSpec((GROUP, dv), lambda b, c: (rows(b, c), 1)),
                  pl.BlockSpec((GROUP, dv), lambda b, c: (rows(b, c), 2)),
                  pl.BlockSpec((GROUP, HEAD), lambda b, c: (rows(b, c), 64)),
                  pl.BlockSpec((1, HEAD), lambda b, c: (0, 0)),
                  pl.BlockSpec((1, dv), lambda b, c: (0, 0)),
                  pl.BlockSpec((None, ML_HEADS, HEAD, ML_DV), lambda b, c: (b, 0, 0, 0)),
                  pl.BlockSpec((None, ML_HEADS, HEAD), lambda b, c: (b, 0, 0)),
                  pl.BlockSpec((None, 1, HEAD), lambda b, c: (b, 0, 0))],
        out_specs=[pl.BlockSpec((GROUP, dv), lambda b, c: (b * nchunks + c, 0)),
                   pl.BlockSpec((None, ML_HEADS, HEAD, ML_DV), lambda b, c: (b, 0, 0, 0)),
                   pl.BlockSpec((None, ML_HEADS, HEAD), lambda b, c: (b, 0, 0)),
                   pl.BlockSpec((None, 1, HEAD), lambda b, c: (b, 0, 0))],
        out_shape=[jax.ShapeDtypeStruct((batch * nchunks * GROUP, dv), BF16),
                   jax.ShapeDtypeStruct((batch, ML_HEADS, HEAD, ML_DV), F32),
                   jax.ShapeDtypeStruct((batch, ML_HEADS, HEAD), F32),
                   jax.ShapeDtypeStruct((batch, 1, HEAD), F32)],
        compiler_params=_cp("parallel", "arbitrary"),
        name="mlstm",
    )(main, main, main, main, tail, bias, nw, c0, n0, m0)


def _merge_kernel(ysb_ref, yml_ref, w0_ref, w1_ref, gsb_ref, gml_ref, o_ref):
    u_sb = jnp.dot(ysb_ref[...], w0_ref[...].astype(BF16), preferred_element_type=F32)
    u_ml = jnp.dot(yml_ref[...], w1_ref[...].astype(BF16), preferred_element_type=F32)
    o_ref[...] = (jax.nn.sigmoid(gsb_ref[...]) * u_sb + jax.nn.sigmoid(gml_ref[...]) * u_ml).astype(o_ref.dtype)


def _merge(y_sb, y_ml, w_branch, tail):
    t, k = y_sb.shape
    d = w_branch.shape[2]
    tm = _pick(t, (1024, 512, 384, 256, 128))
    tn = _pick(d, (512, 256, 128))
    nj = d // tn
    return pl.pallas_call(
        _merge_kernel,
        grid=(t // tm, nj),
        in_specs=[pl.BlockSpec((tm, k), lambda i, j: (i, 0)),
                  pl.BlockSpec((tm, k), lambda i, j: (i, 0)),
                  pl.BlockSpec((None, k, tn), lambda i, j: (0, 0, j)),
                  pl.BlockSpec((None, k, tn), lambda i, j: (1, 0, j)),
                  pl.BlockSpec((tm, tn), lambda i, j: (i, j)),
                  pl.BlockSpec((tm, tn), lambda i, j: (i, nj + j))],
        out_specs=pl.BlockSpec((tm, tn), lambda i, j: (i, j)),
        out_shape=jax.ShapeDtypeStruct((t, d), BF16),
        compiler_params=_cp("parallel", "parallel"),
        name="merge",
    )(y_sb, y_ml, w_branch, w_branch, tail, tail)


def _oproj_kernel(a_ref, w_ref, x_ref, g_ref, o_ref):
    mix = jnp.dot(a_ref[...], w_ref[...].astype(BF16), preferred_element_type=F32)
    tm, tn = mix.shape
    x3 = x_ref[...].reshape(tm // GROUP, GROUP, tn) + g_ref[...] * mix.reshape(tm // GROUP, GROUP, tn)
    o_ref[...] = x3.reshape(tm, tn)


def _oproj(merged, w_o, x, gate):
    t, k = merged.shape
    d = w_o.shape[1]
    tm = _pick(t, (1024, 512, 384, 256, 128))
    tn = _pick(d, (512, 256, 128))
    return pl.pallas_call(
        _oproj_kernel,
        grid=(t // tm, d // tn),
        in_specs=[pl.BlockSpec((tm, k), lambda i, j: (i, 0)),
                  pl.BlockSpec((k, tn), lambda i, j: (0, j)),
                  pl.BlockSpec((tm, tn), lambda i, j: (i, j)),
                  pl.BlockSpec((tm // GROUP, 1, tn), lambda i, j: (i, 0, j))],
        out_specs=pl.BlockSpec((tm, tn), lambda i, j: (i, j)),
        out_shape=jax.ShapeDtypeStruct((t, d), F32),
        compiler_params=_cp("parallel", "parallel"),
        name="oproj",
    )(merged, w_o, x, gate)


def _pack_halves(h):
    d = h.shape[1]
    lo = pltpu.bitcast(h[:, :d // 2].astype(BF16).astype(F32), U32)
    hi = pltpu.bitcast(h[:, d // 2:].astype(BF16).astype(F32), U32)
    return (hi & jnp.uint32(0xFFFF0000)) | (lo >> 16)


def _unpack_halves(w):
    lo = pltpu.bitcast(w << 16, F32).astype(BF16)
    hi = pltpu.bitcast(w & jnp.uint32(0xFFFF0000), F32).astype(BF16)
    return lo, hi


def _modnorm2_kernel(x_ref, nw_ref, sc_ref, sh_ref, whi_ref, wlo_ref, hp_ref, lg_ref):
    h = _modulated(x_ref[...], nw_ref[...], sc_ref[...], sh_ref[...])
    hp_ref[...] = _pack_halves(h)
    h_hi = h.astype(BF16)
    h_lo = (h - h_hi.astype(F32)).astype(BF16)
    whi = whi_ref[...]
    lg_ref[...] = (jnp.dot(h_hi, whi, preferred_element_type=F32)
                   + jnp.dot(h_hi, wlo_ref[...], preferred_element_type=F32)
                   + jnp.dot(h_lo, whi, preferred_element_type=F32))


def _modnorm2(x, nw, sc, sh, w_router):
    t, d = x.shape
    e = w_router.shape[1]
    tm = _pick(t, (256, 128, 64))
    g = tm // GROUP
    whi = w_router.astype(BF16)
    wlo = (w_router - whi.astype(F32)).astype(BF16)
    return pl.pallas_call(
        _modnorm2_kernel,
        grid=(t // tm,),
        in_specs=[pl.BlockSpec((tm, d), lambda i: (i, 0)),
                  pl.BlockSpec((1, d), lambda i: (0, 0)),
                  pl.BlockSpec((g, 1, d), lambda i: (i, 0, 0)),
                  pl.BlockSpec((g, 1, d), lambda i: (i, 0, 0)),
                  pl.BlockSpec((d, e), lambda i: (0, 0)),
                  pl.BlockSpec((d, e), lambda i: (0, 0))],
        out_specs=[pl.BlockSpec((tm, d // 2), lambda i: (i, 0)),
                   pl.BlockSpec((tm, e), lambda i: (i, 0))],
        out_shape=[jax.ShapeDtypeStruct((t, d // 2), U32),
                   jax.ShapeDtypeStruct((t, e), F32)],
        compiler_params=_cp("parallel"),
        name="modnorm2_router",
    )(x, nw.reshape(1, d), sc, sh, whi, wlo)


def _route_kernel(lg_ref, bias_ref, idx_ref, w_ref):
    s = jax.nn.sigmoid(lg_ref[...])
    tm, e = s.shape
    per = e // N_GROUPS
    sb = s + bias_ref[...]
    lane = lax.broadcasted_iota(I32, (tm, e), 1)
    grp = lane // per
    neg = -jnp.inf

    gscore = []
    for g in range(N_GROUPS):
        vals = jnp.where(grp == g, sb, neg)
        m1 = jnp.max(vals, axis=-1, keepdims=True)
        first = jnp.min(jnp.where(vals == m1, lane, e), axis=-1, keepdims=True)
        m2 = jnp.max(jnp.where(lane == first, neg, vals), axis=-1, keepdims=True)
        gscore.append(m1 + m2)

    emask = jnp.zeros((tm, e), jnp.bool_)
    for g in range(N_GROUPS):
        rank = jnp.zeros((tm, 1), I32)
        for o in range(N_GROUPS):
            if o == g:
                continue
            ahead = (gscore[o] > gscore[g]) | ((gscore[o] == gscore[g]) if o < g else False)
            rank = rank + jnp.where(ahead, 1, 0)
        emask = emask | ((grp == g) & (rank < TOPK_GROUPS))

    vals = jnp.where(emask, sb, neg)
    out_lane = lax.broadcasted_iota(I32, (tm, TOP_K), 1)
    idx_out = jnp.zeros((tm, TOP_K), I32)
    w_out = jnp.zeros((tm, TOP_K), F32)
    for kk in range(TOP_K):
        mx = jnp.max(vals, axis=-1, keepdims=True)
        first = jnp.min(jnp.where(vals == mx, lane, e), axis=-1, keepdims=True)
        hit = lane == first
        wk = jnp.sum(jnp.where(hit, s, 0.0), axis=-1, keepdims=True)
        vals = jnp.where(hit, neg, vals)
        idx_out = jnp.where(out_lane == kk, first, idx_out)
        w_out = jnp.where(out_lane == kk, wk, w_out)
    idx_ref[...] = idx_out
    w_ref[...] = w_out / jnp.sum(w_out, axis=-1, keepdims=True) * ROUTED_SCALE


def _route(logits, router_bias):
    t, e = logits.shape
    tm = _pick(t, (256, 128, 64))
    return pl.pallas_call(
        _route_kernel,
        grid=(t // tm,),
        in_specs=[pl.BlockSpec((tm, e), lambda i: (i, 0)),
                  pl.BlockSpec((1, e), lambda i: (0, 0))],
        out_specs=[pl.BlockSpec((tm, TOP_K), lambda i: (i, 0)),
                   pl.BlockSpec((tm, TOP_K), lambda i: (i, 0))],
        out_shape=[jax.ShapeDtypeStruct((t, TOP_K), I32),
                   jax.ShapeDtypeStruct((t, TOP_K), F32)],
        compiler_params=_cp("parallel"),
        name="route",
    )(logits, router_bias.reshape(1, e))


def _dispatch_kernel(tok_ref, x_hbm, o_ref, sem):
    n = o_ref.shape[0]

    def copy(r):
        return pltpu.make_async_copy(x_hbm.at[pl.ds(tok_ref[0, 0, r], 1)], o_ref.at[pl.ds(r, 1)], sem)

    def start(r, _):
        copy(r).start()
        return 0

    def wait(r, _):
        copy(r).wait()
        return 0

    lax.fori_loop(0, n, start, 0, unroll=8)
    lax.fori_loop(0, n, wait, 0, unroll=8)


def _dispatch(hp, tok_sorted):
    t, w = hp.shape
    a = tok_sorted.shape[0]
    n = _pick(a, (GATHER_ROWS, 256, 128))
    return pl.pallas_call(
        _dispatch_kernel,
        grid=(a // n,),
        in_specs=[pl.BlockSpec((1, 1, n), lambda i: (i, 0, 0), memory_space=pltpu.SMEM),
                  pl.BlockSpec(memory_space=pl.ANY)],
        out_specs=pl.BlockSpec((n, w), lambda i: (i, 0)),
        out_shape=jax.ShapeDtypeStruct((a, w), U32),
        scratch_shapes=[pltpu.SemaphoreType.DMA(())],
        compiler_params=_cp("arbitrary"),
        name="dispatch",
    )(tok_sorted.reshape(a // n, 1, n), hp)


def _pair_rows(lo_ref, hi_ref, shape):
    i = pl.program_id(0)
    rows = lax.broadcasted_iota(I32, shape, 0)
    return (rows >= lo_ref[i]) & (rows < hi_ref[i])


def _stage_expert_weights(pe_ref, fe_ref, nxt_ref, w_hbm, w_f32, w_bf, sems):
    i = pl.program_id(0)

    def copies(e):
        return [pltpu.make_async_copy(src.at[e], dst, sems.at[n])
                for n, (src, dst) in enumerate(zip(w_hbm, w_f32))]

    @pl.when(i == 0)
    def _():
        for c in copies(pe_ref[0]):
            c.start()

    @pl.when(fe_ref[i] == 1)
    def _():
        for c in copies(pe_ref[i]):
            c.wait()
        for src, dst in zip(w_f32, w_bf):
            dst[...] = src[...].astype(BF16)

        @pl.when(nxt_ref[i] >= 0)
        def _():
            for c in copies(nxt_ref[i]):
                c.start()


def _expert_up_kernel(pe_ref, pb_ref, lo_ref, hi_ref, fe_ref, fb_ref, nxt_ref, x_ref, wg_hbm, wu_hbm, h_ref,
                      wg_f32, wu_f32, wg_s, wu_s, sems):
    i = pl.program_id(0)
    _stage_expert_weights(pe_ref, fe_ref, nxt_ref, (wg_hbm, wu_hbm), (wg_f32, wu_f32), (wg_s, wu_s), sems)

    @pl.when(hi_ref[i] > lo_ref[i])
    def _():
        lo, hi = _unpack_halves(x_ref[...])
        half = lo.shape[1]
        gt = (jnp.dot(lo, wg_s[:half, :], preferred_element_type=F32)
              + jnp.dot(hi, wg_s[half:, :], preferred_element_type=F32))
        up = (jnp.dot(lo, wu_s[:half, :], preferred_element_type=F32)
              + jnp.dot(hi, wu_s[half:, :], preferred_element_type=F32))
        hv = (gt * jax.nn.sigmoid(gt) * up).astype(h_ref.dtype)
        mine = _pair_rows(lo_ref, hi_ref, hv.shape)

        @pl.when(fb_ref[i] == 1)
        def _():
            h_ref[...] = jnp.where(mine, hv, jnp.zeros_like(hv))

        @pl.when(fb_ref[i] == 0)
        def _():
            h_ref[...] = jnp.where(mine, hv, h_ref[...])


def _expert_down_kernel(pe_ref, pb_ref, lo_ref, hi_ref, fe_ref, fb_ref, nxt_ref, h_ref, wd_hbm, o_ref,
                        wd_f32, wd_s, sems):
    i = pl.program_id(0)
    _stage_expert_weights(pe_ref, fe_ref, nxt_ref, (wd_hbm,), (wd_f32,), (wd_s,), sems)

    @pl.when(hi_ref[i] > lo_ref[i])
    def _():
        ov = jnp.dot(h_ref[...], wd_s[...], preferred_element_type=F32)
        mine = _pair_rows(lo_ref, hi_ref, ov.shape)

        @pl.when(fb_ref[i] == 1)
        def _():
            o_ref[...] = jnp.where(mine, ov, 0.0)

        @pl.when(fb_ref[i] == 0)
        def _():
            o_ref[...] = jnp.where(mine, ov, o_ref[...])


def _experts(xs, pairs, w_gate, w_up, w_down):
    a, half = xs.shape
    ne, d, de = w_gate.shape
    npairs = pairs[0].shape[0]

    def rows(i, pe, pb, *_):
        return (pb[i], 0)

    hbm = pl.BlockSpec(memory_space=pl.ANY)
    h = pl.pallas_call(
        _expert_up_kernel,
        grid_spec=pltpu.PrefetchScalarGridSpec(
            num_scalar_prefetch=len(pairs),
            grid=(npairs,),
            in_specs=[pl.BlockSpec((MOE_BLOCK, half), rows), hbm, hbm],
            out_specs=pl.BlockSpec((MOE_BLOCK, de), rows),
            scratch_shapes=[pltpu.VMEM((d, de), F32), pltpu.VMEM((d, de), F32),
                            pltpu.VMEM((d, de), BF16), pltpu.VMEM((d, de), BF16),
                            pltpu.SemaphoreType.DMA((2,))]),
        out_shape=jax.ShapeDtypeStruct((a, de), BF16),
        compiler_params=_cp("arbitrary"),
        name="expert_up",
    )(*pairs, xs, w_gate, w_up)

    return pl.pallas_call(
        _expert_down_kernel,
        grid_spec=pltpu.PrefetchScalarGridSpec(
            num_scalar_prefetch=len(pairs),
            grid=(npairs,),
            in_specs=[pl.BlockSpec((MOE_BLOCK, de), rows), hbm],
            out_specs=pl.BlockSpec((MOE_BLOCK, d), rows),
            scratch_shapes=[pltpu.VMEM((de, d), F32), pltpu.VMEM((de, d), BF16),
                            pltpu.SemaphoreType.DMA((1,))]),
        out_shape=jax.ShapeDtypeStruct((a, d), F32),
        compiler_params=_cp("arbitrary"),
        name="expert_down",
    )(*pairs, h, w_down)


def _shared_kernel(x_ref, wg_ref, wu_ref, wd_ref, o_ref):
    lo, hi = _unpack_halves(x_ref[...])
    half = lo.shape[1]
    gt = (jnp.dot(lo, wg_ref[:half, :], preferred_element_type=F32)
          + jnp.dot(hi, wg_ref[half:, :], preferred_element_type=F32))
    up = (jnp.dot(lo, wu_ref[:half, :], preferred_element_type=F32)
          + jnp.dot(hi, wu_ref[half:, :], preferred_element_type=F32))
    hh = (gt * jax.nn.sigmoid(gt) * up).astype(BF16)
    o_ref[...] = jnp.dot(hh, wd_ref[...], preferred_element_type=F32)


def _shared(hp, w_gate, w_up, w_down):
    t, half = hp.shape
    d, de = w_gate.shape
    tm = _pick(t, (256, 128, 64))
    return pl.pallas_call(
        _shared_kernel,
        grid=(t // tm,),
        in_specs=[pl.BlockSpec((tm, half), lambda i: (i, 0)),
                  pl.BlockSpec((d, de), lambda i: (0, 0)),
                  pl.BlockSpec((d, de), lambda i: (0, 0)),
                  pl.BlockSpec((de, d), lambda i: (0, 0))],
        out_specs=pl.BlockSpec((tm, d), lambda i: (i, 0)),
        out_shape=jax.ShapeDtypeStruct((t, d), F32),
        compiler_params=_cp("parallel"),
        name="shared_expert",
    )(hp, w_gate.astype(BF16), w_up.astype(BF16), w_down.astype(BF16))


def _combine_kernel(dest_ref, w_ref, sh_ref, x_ref, g_ref, nw_ref, y_hbm, o_ref, buf, sem):
    tt = x_ref.shape[0]

    def copy(t, kk):
        return pltpu.make_async_copy(y_hbm.at[pl.ds(dest_ref[0, 0, t * TOP_K + kk], 1)],
                                     buf.at[kk, pl.ds(t, 1)], sem)

    def start(t, _):
        for kk in range(TOP_K):
            copy(t, kk).start()
        return 0

    def wait(t, _):
        for kk in range(TOP_K):
            copy(t, kk).wait()
        return 0

    lax.fori_loop(0, tt, start, 0)
    lax.fori_loop(0, tt, wait, 0)

    w = w_ref[...]
    ff = sh_ref[...]
    for kk in range(TOP_K):
        ff = ff + w[:, kk:kk + 1] * buf[kk]
    x = x_ref[...] + g_ref[0] * ff
    o_ref[...] = x * lax.rsqrt(jnp.mean(x * x, axis=-1, keepdims=True) + EPS) * nw_ref[...]


def _combine(y_sorted, dest, gw, shared, x, gate, nw):
    t, d = x.shape
    tt = GROUP
    return pl.pallas_call(
        _combine_kernel,
        grid=(t // tt,),
        in_specs=[pl.BlockSpec((1, 1, tt * TOP_K), lambda i: (i, 0, 0), memory_space=pltpu.SMEM),
                  pl.BlockSpec((tt, TOP_K), lambda i: (i, 0)),
                  pl.BlockSpec((tt, d), lambda i: (i, 0)),
                  pl.BlockSpec((tt, d), lambda i: (i, 0)),
                  pl.BlockSpec((1, 1, d), lambda i: (i, 0, 0)),
                  pl.BlockSpec((1, d), lambda i: (0, 0)),
                  pl.BlockSpec(memory_space=pl.ANY)],
        out_specs=pl.BlockSpec((tt, d), lambda i: (i, 0)),
        out_shape=jax.ShapeDtypeStruct((t, d), F32),
        scratch_shapes=[pltpu.VMEM((TOP_K, tt, d), F32), pltpu.SemaphoreType.DMA(())],
        compiler_params=_cp("arbitrary"),
        name="combine_final",
    )(dest.reshape(t // tt, 1, tt * TOP_K), gw, shared, x, gate, nw.reshape(1, d), y_sorted)


def _dispatch_plan(eidx, n_experts):
    t = eidx.shape[0]
    a = t * TOP_K
    assert a % MOE_BLOCK == 0
    flat_e = eidx.reshape(-1)
    order = jnp.argsort(flat_e).astype(I32)
    dest = jnp.argsort(order).astype(I32)
    tok_sorted = order // TOP_K
    counts = jnp.zeros((n_experts,), I32).at[flat_e].add(1)
    cend = jnp.cumsum(counts)
    cstart = cend - counts
    first_blk = cstart // MOE_BLOCK
    nblk = jnp.where(counts > 0, (cend - 1) // MOE_BLOCK - first_blk + 1, 0)
    pend = jnp.cumsum(nblk)
    pstart = pend - nblk
    npairs = a // MOE_BLOCK + n_experts
    p = jnp.arange(npairs, dtype=I32)
    used = p < pend[-1]
    pc = jnp.minimum(p, pend[-1] - 1)
    pe = jnp.minimum(jnp.searchsorted(pend, pc, side='right'), n_experts - 1).astype(I32)
    pb = first_blk[pe] + pc - pstart[pe]
    lo = jnp.maximum(cstart[pe] - pb * MOE_BLOCK, 0)
    hi = jnp.where(used, jnp.minimum(cend[pe] - pb * MOE_BLOCK, MOE_BLOCK), lo)
    prev = lambda v: jnp.concatenate([jnp.full((1,), -1, I32), v[:-1]])
    fe = jnp.where(used & (pe != prev(pe)), 1, 0)
    fb = jnp.where(used & (pb != prev(pb)), 1, 0)
    ids = jnp.arange(n_experts, dtype=I32)
    at_or_after = lax.cummin(jnp.where(counts > 0, ids, n_experts), reverse=True)
    after = jnp.concatenate([at_or_after[1:], jnp.full((1,), n_experts, I32)])
    nxt = jnp.where(after < n_experts, after, -1)[pe]
    return dest, tok_sorted, tuple(v.astype(I32) for v in (pe, pb, lo, hi, fe, fb, nxt))


def kernel(x_prompt, x_sample, cache_sb_k, cache_sb_v, state_mlstm_c, state_mlstm_n, state_mlstm_m, c_prompt, c_sample, w_ada, b_ada, norm1_w, w_in, b_ig, b_fg, ml_norm_w, w_branch, w_o, norm2_w, w_router, router_bias, w_e_gate, w_e_up, w_e_down, w_s_gate, w_s_up, w_s_down, norm_f_w):
    bp, sp, d = x_prompt.shape
    bs, ss, _ = x_sample.shape
    depth = w_ada.shape[0]
    assert depth == 1 and ss == GROUP and sp % SB_TQ == 0
    tp, ts = bp * sp, bs * ss
    t = tp + ts
    d_sb = SB_HEADS * HEAD
    d_mlk = ML_HEADS * HEAD
    d_ml = ML_HEADS * ML_DV
    n_main = 3 * d_sb + 2 * d_mlk + 2 * d_ml
    n_experts = w_router.shape[2]
    l = 0

    x = jnp.concatenate([x_prompt.reshape(tp, d), x_sample.reshape(ts, d)], axis=0)

    c_all = jnp.concatenate([c_prompt, c_sample, jnp.zeros((-(bp + bs) % 8, d), F32)], axis=0)
    mod = _ada(c_all, w_ada[l], b_ada[l])
    mod_g = jnp.concatenate([jnp.repeat(mod[:bp], sp // GROUP, axis=0), mod[bp:bp + bs]], axis=0)
    sh1, sc1, g1, sh2, sc2, g2 = [m.reshape(t // GROUP, 1, d) for m in jnp.split(mod_g, 6, axis=-1)]

    hm = _modnorm(x, norm1_w[l], sc1, sh1)
    qkv, k_sb, v_sb, main = _in_proj(hm, w_in[l][:, :n_main].astype(BF16), 3 * d_sb, n_main - 3 * d_sb)
    w_tail = jnp.concatenate([w_in[l][:, n_main + 2 * ML_HEADS:], w_in[l][:, n_main:n_main + 2 * ML_HEADS],
                              jnp.zeros((d, HEAD - 2 * ML_HEADS), F32)], axis=1).astype(BF16)
    tail = _matmul(hm, w_tail, w_tail.shape[1], "in_proj_tail")

    y_sb_p = _sb_prompt(qkv, bp, sp)
    y_sb_s = _sb_sample(qkv, cache_sb_k[l].reshape(bs, -1, d_sb), cache_sb_v[l].reshape(bs, -1, d_sb),
                        tp // GROUP, bs, ss)
    y_sb = jnp.concatenate([y_sb_p, y_sb_s], axis=0)

    gate_bias = jnp.concatenate([b_ig[l], b_fg[l], jnp.zeros((HEAD - 2 * ML_HEADS,), F32)]).reshape(1, HEAD)
    nw_ml = ml_norm_w[l].reshape(1, d_ml)
    y_ml_p, c_p, n_p, m_p = _mlstm(main, tail, gate_bias, nw_ml,
                                   jnp.zeros((bp, ML_HEADS, HEAD, ML_DV), F32),
                                   jnp.zeros((bp, ML_HEADS, HEAD), F32),
                                   jnp.zeros((bp, 1, HEAD), F32), 0, bp, sp // GROUP)
    m0_s = jnp.pad(state_mlstm_m[l], ((0, 0), (0, HEAD - ML_HEADS))).reshape(bs, 1, HEAD)
    y_ml_s, c_s, n_s, m_s = _mlstm(main, tail, gate_bias, nw_ml, state_mlstm_c[l], state_mlstm_n[l], m0_s,
                                   tp // GROUP, bs, 1)
    y_ml = jnp.concatenate([y_ml_p, y_ml_s], axis=0)

    merged = _merge(y_sb, y_ml, w_branch[l], tail)
    x1 = _oproj(merged, w_o[l], x, g1)

    hp, logits = _modnorm2(x1, norm2_w[l], sc2, sh2, w_router[l])
    eidx, gw = _route(logits, router_bias[l])
    dest, tok_sorted, pairs = _dispatch_plan(eidx, n_experts)
    xs = _dispatch(hp, tok_sorted)
    y_sorted = _experts(xs, pairs, w_e_gate[l], w_e_up[l], w_e_down[l])
    shared = _shared(hp, w_s_gate[l], w_s_up[l], w_s_down[l])
    y = _combine(y_sorted, dest, gw, shared, x1, g2, norm_f_w)

    return (y[:tp].reshape(bp, sp, d), y[tp:].reshape(bs, ss, d),
            k_sb[:tp].reshape(1, bp, sp, SB_HEADS, HEAD), v_sb[:tp].reshape(1, bp, sp, SB_HEADS, HEAD),
            c_p[None], n_p[None], m_p[:, 0, :ML_HEADS][None],
            k_sb[tp:].reshape(1, bs, ss, SB_HEADS, HEAD), v_sb[tp:].reshape(1, bs, ss, SB_HEADS, HEAD),
            c_s[None], n_s[None], m_s[:, 0, :ML_HEADS][None])
```

```python
from functools import partial

import jax
import jax.numpy as jnp
from jax import lax
from jax.experimental import pallas as pl
from jax.experimental.pallas import tpu as pltpu

F32 = jnp.float32
BF16 = jnp.bfloat16
U32 = jnp.uint32
I32 = jnp.int32

EPS = 1e-6
GROUP = 64
HEAD = 128
SB_HEADS = 16
ML_HEADS = 8
ML_DV = 256
N_GROUPS = 8
TOPK_GROUPS = 4
TOP_K = 8
ROUTED_SCALE = 2.5
MOE_BLOCK = 128
CAST_CHUNK_ELEMS = 64 * 1024
SB_TQ = 512
SB_TK = 128
SB_TKB = 512
SB_CACHE_BLOCK = SB_TKB
SB_SAMPLE_UNROLL = 8
GATHER_ROWS = 512
VMEM_LIMIT = 56 * 1024 * 1024


def _cp(*sem):
    return pltpu.CompilerParams(dimension_semantics=sem, vmem_limit_bytes=VMEM_LIMIT)


def _pick(n, prefs):
    for p in prefs:
        if n % p == 0:
            return p
    return n


def _ada_kernel(c_ref, w_ref, b_ref, o_ref):
    c = c_ref[...]
    a = (c * jax.nn.sigmoid(c)).astype(BF16)
    o_ref[...] = jnp.dot(a, w_ref[...].astype(BF16), preferred_element_type=F32) + b_ref[...]


def _ada(c, w, b):
    m, k = c.shape
    n = w.shape[1]
    tn = _pick(n, (512, 256, 128))
    return pl.pallas_call(
        _ada_kernel,
        grid=(n // tn,),
        in_specs=[pl.BlockSpec((m, k), lambda j: (0, 0)),
                  pl.BlockSpec((k, tn), lambda j: (0, j)),
                  pl.BlockSpec((1, tn), lambda j: (0, j))],
        out_specs=pl.BlockSpec((m, tn), lambda j: (0, j)),
        out_shape=jax.ShapeDtypeStruct((m, n), F32),
        compiler_params=_cp("parallel"),
        name="ada",
    )(c, w, b.reshape(1, n))


def _modulated(x, nw, sc, sh):
    tm, d = x.shape
    ms = jnp.mean(x * x, axis=-1, keepdims=True)
    y = x * lax.rsqrt(ms + EPS) * nw
    y3 = y.reshape(tm // GROUP, GROUP, d) * (1.0 + sc) + sh
    return y3.reshape(tm, d)


def _modnorm_kernel(x_ref, nw_ref, sc_ref, sh_ref, o_ref):
    o_ref[...] = _modulated(x_ref[...], nw_ref[...], sc_ref[...], sh_ref[...]).astype(o_ref.dtype)


def _modnorm(x, nw, sc, sh):
    t, d = x.shape
    tm = _pick(t, (256, 128, 64))
    g = tm // GROUP
    return pl.pallas_call(
        _modnorm_kernel,
        grid=(t // tm,),
        in_specs=[pl.BlockSpec((tm, d), lambda i: (i, 0)),
                  pl.BlockSpec((1, d), lambda i: (0, 0)),
                  pl.BlockSpec((g, 1, d), lambda i: (i, 0, 0)),
                  pl.BlockSpec((g, 1, d), lambda i: (i, 0, 0))],
        out_specs=pl.BlockSpec((tm, d), lambda i: (i, 0)),
        out_shape=jax.ShapeDtypeStruct((t, d), BF16),
        compiler_params=_cp("parallel"),
        name="modnorm1",
    )(x, nw.reshape(1, d), sc, sh)


def _mm_kernel(a_ref, w_ref, o_ref):
    o_ref[...] = jnp.dot(a_ref[...], w_ref[...].astype(BF16),
                         preferred_element_type=F32).astype(o_ref.dtype)


def _matmul(a, w, n, name):
    m, k = a.shape
    tm = _pick(m, (1024, 512, 384, 256, 128))
    tn = _pick(n, (512, 640, 256, 128))
    return pl.pallas_call(
        _mm_kernel,
        grid=(m // tm, n // tn),
        in_specs=[pl.BlockSpec((tm, k), lambda i, j: (i, 0)),
                  pl.BlockSpec((k, tn), lambda i, j: (0, j))],
        out_specs=pl.BlockSpec((tm, tn), lambda i, j: (i, j)),
        out_shape=jax.ShapeDtypeStruct((m, n), F32),
        compiler_params=_cp("parallel", "parallel"),
        name=name,
    )(a, w)


IN_TN = 512


def _in_proj_kernel(a_ref, w_ref, qkv_ref, f32_ref, *, n_q_tiles, n_sb_tiles):
    j = pl.program_id(1)
    res = jnp.dot(a_ref[...], w_ref[...], preferred_element_type=F32)

    @pl.when(j < n_sb_tiles)
    def _():
        qkv_ref[...] = res.astype(qkv_ref.dtype)

    @pl.when(j >= n_q_tiles)
    def _():
        f32_ref[...] = res


def _in_proj(a, w, n_q, n_sb):
    m, k = a.shape
    n = w.shape[1]
    tm = _pick(m, (1024, 512, 384, 256, 128))
    tn = IN_TN
    nq, nsb = n_q // tn, n_sb // tn
    return pl.pallas_call(
        partial(_in_proj_kernel, n_q_tiles=nq, n_sb_tiles=nsb),
        grid=(m // tm, n // tn),
        in_specs=[pl.BlockSpec((tm, k), lambda i, j: (i, 0)),
                  pl.BlockSpec((k, tn), lambda i, j: (0, j))],
        out_specs=[pl.BlockSpec((tm, tn), lambda i, j: (i, jnp.minimum(j, nsb - 1))),
                   pl.BlockSpec((tm, tn), lambda i, j: (i, jnp.maximum(j - nq, 0)))],
        out_shape=[jax.ShapeDtypeStruct((m, n_sb), BF16),
                   jax.ShapeDtypeStruct((m, n - n_q), F32)],
        compiler_params=_cp("parallel", "arbitrary"),
        name="in_proj_main",
    )(a, w)


def _softplus(z):
    return jnp.maximum(z, 0.0) + jnp.log(1.0 + jnp.exp(-jnp.abs(z)))


def _sb_step(q, k, v, carry, acc, tri, mask):
    z = lax.dot_general(q, k, (((1,), (1,)), ((), ())), preferred_element_type=F32) * (HEAD ** -0.5)
    sp = _softplus(z)
    spm = sp if mask is None else jnp.where(mask, sp, 0.0)
    hi = spm.astype(BF16)
    mid = (spm - hi.astype(F32)).astype(BF16)
    excl = (jnp.dot(hi, tri, preferred_element_type=F32)
            + jnp.dot(mid, tri, preferred_element_type=F32) + carry)
    a = jnp.exp(z - sp - excl)
    if mask is not None:
        a = jnp.where(mask, a, 0.0)
    acc = acc + jnp.dot(a.astype(BF16), v, preferred_element_type=F32)
    carry = carry + jnp.sum(spm, axis=1, keepdims=True)
    return carry, acc


def _tri(n):
    r = lax.broadcasted_iota(I32, (n, n), 0)
    c = lax.broadcasted_iota(I32, (n, n), 1)
    return jnp.where(r > c, 1.0, 0.0).astype(BF16)


def _tri2():
    t = _tri(SB_TK)
    return jnp.concatenate([t, t], axis=0)


def _sb_wide_step(q, k, v, carry, acc, tri2, mask):
    if not isinstance(q, (list, tuple)):
        q, k, v = [q], [k], [v]
    rows = q[0].shape[0]
    tq = rows * len(q)
    nsub = k[0].shape[0] // SB_TK
    z = jnp.concatenate([lax.dot_general(qh, kh, (((1,), (1,)), ((), ())), preferred_element_type=F32)
                         for qh, kh in zip(q, k)], axis=0) * (HEAD ** -0.5)
    sp = _softplus(z)
    spm = sp if mask is None else jnp.where(mask, sp, 0.0)
    hi = spm.astype(BF16)
    mid = (spm - hi.astype(F32)).astype(BF16)
    cols = [slice(j * SB_TK, (j + 1) * SB_TK) for j in range(nsub)]
    lhs = jnp.concatenate([jnp.concatenate([hi[:, c], mid[:, c]], axis=1) for c in cols], axis=0)
    within = jnp.dot(lhs, tri2, preferred_element_type=F32)
    excl = [None] * nsub
    off = carry
    for j in range(nsub - 1, -1, -1):
        excl[j] = within[j * tq:(j + 1) * tq] + off
        off = off + jnp.sum(spm[:, cols[j]], axis=1, keepdims=True)
    a = jnp.exp(z - sp - jnp.concatenate(excl, axis=1))
    if mask is not None:
        a = jnp.where(mask, a, 0.0)
    ab = a.astype(BF16)
    acc = acc + jnp.concatenate([jnp.dot(ab[g * rows:(g + 1) * rows], vh, preferred_element_type=F32)
                                 for g, vh in enumerate(v)], axis=0)
    return off, acc


def _sb_prompt_kernel(q_ref, k_ref, v_ref, o_ref):
    qi = pl.program_id(2)
    tq, tkb = SB_TQ, SB_TKB
    q = q_ref[...]
    tri2 = _tri2()

    def load(kb):
        rows = pl.ds(pl.multiple_of(kb * tkb, tkb), tkb)
        return k_ref[rows, :], v_ref[rows, :]

    k, v = load(qi)
    causal = lax.broadcasted_iota(I32, (tq, tkb), 1) < lax.broadcasted_iota(I32, (tq, tkb), 0)
    carry, acc = _sb_wide_step(q, k, v, jnp.zeros((tq, 1), F32), jnp.zeros((tq, HEAD), F32), tri2, causal)

    def body(i, ca):
        k, v = load(qi - 1 - i)
        return _sb_wide_step(q, k, v, ca[0], ca[1], tri2, None)

    carry, acc = lax.fori_loop(0, qi, body, (carry, acc))
    o_ref[...] = acc.astype(o_ref.dtype)


def _sb_prompt(qkv, batch, seq):
    assert SB_TQ == SB_TKB and seq % SB_TQ == 0
    nq = seq // SB_TQ
    return pl.pallas_call(
        _sb_prompt_kernel,
        grid=(batch, SB_HEADS, nq),
        in_specs=[pl.BlockSpec((SB_TQ, HEAD), lambda b, h, i: (b * nq + i, h)),
                  pl.BlockSpec((seq, HEAD), lambda b, h, i: (b, SB_HEADS + h)),
                  pl.BlockSpec((seq, HEAD), lambda b, h, i: (b, 2 * SB_HEADS + h))],
        out_specs=pl.BlockSpec((SB_TQ, HEAD), lambda b, h, i: (b * nq + i, h)),
        out_shape=jax.ShapeDtypeStruct((batch * seq, SB_HEADS * HEAD), BF16),
        compiler_params=_cp("parallel", "parallel", "arbitrary"),
        name="sb_prompt",
    )(qkv, qkv, qkv)


def _sb_sample_kernel(q_ref, kn_ref, vn_ref, kc_hbm, vc_hbm, o_ref, carry_ref, acc_ref, kbuf, vbuf, sems):
    b = pl.program_id(0)
    s = pl.program_id(1)
    nb = pl.num_programs(0)
    ns = pl.num_programs(1)
    ncb = ns - 1
    lq = q_ref.shape[0]
    cb = kbuf.shape[1]

    def copies(bb, blk, slot):
        rows = pl.ds(pl.multiple_of(blk * cb, cb), cb)
        out = []
        for h in range(SB_HEADS):
            cols = pl.ds(h * HEAD, HEAD)
            out.append(pltpu.make_async_copy(kc_hbm.at[bb, rows, h, :], kbuf.at[slot, :, cols], sems.at[0, slot]))
            out.append(pltpu.make_async_copy(vc_hbm.at[bb, rows, h, :], vbuf.at[slot, :, cols], sems.at[1, slot]))
        return out

    seq = b * ncb + s - 1
    slot = lax.rem(jnp.maximum(seq, 0), 2)

    @pl.when((b == 0) & (s == 0))
    def _():
        for c in copies(0, ncb - 1, 0):
            c.start()

    @pl.when(s > 0)
    def _():
        for c in copies(b, ncb - s, slot):
            c.wait()

        @pl.when(s < ncb)
        def _():
            for c in copies(b, ncb - s - 1, 1 - slot):
                c.start()

        @pl.when((s == ncb) & (b + 1 < nb))
        def _():
            for c in copies(b + 1, ncb - 1, 1 - slot):
                c.start()

    kc_ref = kbuf.at[slot]
    vc_ref = vbuf.at[slot]

    @pl.when(s == 0)
    def _():
        tri = _tri(lq)
        mask = (lax.broadcasted_iota(I32, (lq, lq), 1) < lax.broadcasted_iota(I32, (lq, lq), 0))

        def head(h, _):
            cols = pl.ds(pl.multiple_of(h * HEAD, HEAD), HEAD)
            rows = pl.ds(pl.multiple_of(h * lq, lq), lq)
            carry, acc = _sb_step(q_ref[:, cols], kn_ref[:, cols], vn_ref[:, cols],
                                  jnp.zeros((lq, 1), F32), jnp.zeros((lq, HEAD), F32), tri, mask)
            carry_ref[rows, :] = carry
            acc_ref[rows, :] = acc
            return 0

        lax.fori_loop(0, SB_HEADS, head, 0, unroll=2)

    @pl.when(s > 0)
    def _():
        tri2 = _tri2()
        g = SB_SAMPLE_UNROLL

        def heads(i, _):
            cols = [pl.ds(pl.multiple_of((i * g + u) * HEAD, HEAD), HEAD) for u in range(g)]
            rows = pl.ds(pl.multiple_of(i * (g * lq), g * lq), g * lq)
            carry, acc = _sb_wide_step([q_ref[:, c] for c in cols],
                                       [kc_ref[:, c].astype(BF16) for c in cols],
                                       [vc_ref[:, c].astype(BF16) for c in cols],
                                       carry_ref[rows, :], acc_ref[rows, :], tri2, None)
            carry_ref[rows, :] = carry
            acc_ref[rows, :] = acc
            return 0

        lax.fori_loop(0, SB_HEADS // g, heads, 0)

    @pl.when(s == ns - 1)
    def _():
        for h in range(SB_HEADS):
            o_ref[:, h * HEAD:(h + 1) * HEAD] = acc_ref[h * lq:(h + 1) * lq, :].astype(o_ref.dtype)


def _sb_sample(main, cache_k, cache_v, row_block0, batch, lq):
    past = cache_k.shape[1]
    cb = SB_CACHE_BLOCK
    assert past % cb == 0 and past >= cb
    ncb = past // cb
    dsb = SB_HEADS * HEAD
    hbm = pl.BlockSpec(memory_space=pl.ANY)
    return pl.pallas_call(
        _sb_sample_kernel,
        grid=(batch, ncb + 1),
        in_specs=[pl.BlockSpec((lq, dsb), lambda b, s: (row_block0 + b, 0)),
                  pl.BlockSpec((lq, dsb), lambda b, s: (row_block0 + b, 1)),
                  pl.BlockSpec((lq, dsb), lambda b, s: (row_block0 + b, 2)),
                  hbm, hbm],
        out_specs=pl.BlockSpec((lq, dsb), lambda b, s: (b, 0)),
        out_shape=jax.ShapeDtypeStruct((batch * lq, dsb), BF16),
        scratch_shapes=[pltpu.VMEM((SB_HEADS * lq, 1), F32),
                        pltpu.VMEM((SB_HEADS * lq, HEAD), F32),
                        pltpu.VMEM((2, cb, dsb), F32), pltpu.VMEM((2, cb, dsb), F32),
                        pltpu.SemaphoreType.DMA((2, 2))],
        compiler_params=_cp("arbitrary", "arbitrary"),
        name="sb_sample",
    )(main, main, main, cache_k, cache_v)


def _split3(x):
    p0 = x.astype(BF16)
    r = x - p0.astype(F32)
    p1 = r.astype(BF16)
    p2 = (r - p1.astype(F32)).astype(BF16)
    return p0, p1, p2


def _mlstm_kernel(q_ref, k_ref, v_ref, o_ref, if_ref, bias_ref, nw_ref, c0_ref, n0_ref, m0_ref,
                  y_ref, c_s, n_s, m_s):
    ci = pl.program_id(1)
    L = GROUP

    @pl.when(ci == 0)
    def _():
        c_s[...] = c0_ref[...]
        n_s[...] = n0_ref[...]
        m_s[...] = m0_ref[...]

    g = if_ref[...] + bias_ref[...]
    logf = pltpu.roll(jax.nn.log_sigmoid(g), HEAD - ML_HEADS, axis=1)
    row = lax.broadcasted_iota(I32, (L, L), 0)
    col = lax.broadcasted_iota(I32, (L, L), 1)
    tril = row >= col
    tril_b = jnp.where(tril, 1.0, 0.0).astype(BF16)
    b = sum(jnp.dot(tril_b, p, preferred_element_type=F32) for p in _split3(logf))
    b_t = b.T
    i_t = g.T
    m_vec = m_s[...]
    lane = lax.broadcasted_iota(I32, (1, HEAD), 1)
    m_out = m_vec

    for h in range(ML_HEADS):
        q = q_ref[:, h * HEAD:(h + 1) * HEAD].astype(BF16)
        kf = k_ref[:, h * HEAD:(h + 1) * HEAD] * (HEAD ** -0.5)
        k = kf.astype(BF16)
        vf = v_ref[:, h * ML_DV:(h + 1) * ML_DV]
        v = vf.astype(BF16)
        c = c_s[h]
        n = n_s[h:h + 1, :]
        m = m_vec[:, h:h + 1]
        b_col = b[:, h:h + 1]
        i_col = g[:, h:h + 1]
        b_row = b_t[h:h + 1, :]
        i_row = i_t[h:h + 1, :]

        d_log = jnp.where(tril, b_col - b_row + i_row, -jnp.inf)
        inter = b_col + m
        m_t = jnp.maximum(jnp.max(d_log, axis=-1, keepdims=True), inter)
        qk = lax.dot_general(q, k, (((1,), (1,)), ((), ())), preferred_element_type=F32)
        w_intra = jnp.exp(d_log - m_t) * qk
        w_inter = jnp.exp(inter - m_t)
        qc = jnp.dot(q, c.astype(BF16), preferred_element_type=F32)
        num = jnp.dot(w_intra.astype(BF16), v, preferred_element_type=F32) + w_inter * qc
        qn = jnp.sum(q.astype(F32) * n, axis=-1, keepdims=True)
        den = jnp.sum(w_intra, axis=-1, keepdims=True) + w_inter * qn
        hh = num / jnp.maximum(jnp.abs(den), jnp.exp(-m_t))

        b_last = b_col[L - 1:L, :]
        log_src = b_last - b_col + i_col
        m_new = jnp.maximum(b_last + m, jnp.max(log_src, axis=0, keepdims=True))
        src_w = jnp.exp(log_src - m_new)
        car_w = jnp.exp(b_last + m - m_new)
        c_s[h] = car_w * c + lax.dot_general(k, (src_w * vf).astype(BF16), (((0,), (0,)), ((), ())),
                                             preferred_element_type=F32)
        n_s[h:h + 1, :] = car_w * n + jnp.sum(src_w * k.astype(F32), axis=0, keepdims=True)
        m_out = jnp.where(lane == h, m_new, m_out)

        hn = hh * lax.rsqrt(jnp.mean(hh * hh, axis=-1, keepdims=True) + EPS)
        gate = jax.nn.sigmoid(o_ref[:, h * ML_DV:(h + 1) * ML_DV])
        y_ref[:, h * ML_DV:(h + 1) * ML_DV] = (hn * nw_ref[:, h * ML_DV:(h + 1) * ML_DV] * gate).astype(y_ref.dtype)

    m_s[...] = m_out


def _mlstm(main, col0, tail, bias, nw, c0, n0, m0, row_block0, batch, nchunks):
    dk = ML_HEADS * HEAD
    dv = ML_HEADS * ML_DV
    assert col0 % dv == 0
    qb, vb = col0 // dk, col0 // dv + 1

    def rows(b, c):
        return row_block0 + b * nchunks + c

    return pl.pallas_call(
        _mlstm_kernel,
        grid=(batch, nchunks),
        in_specs=[pl.BlockSpec((GROUP, dk), lambda b, c: (rows(b, c), qb)),
                  pl.BlockSpec((GROUP, dk), lambda b, c: (rows(b, c), qb + 1)),
                  pl.BlockSpec((GROUP, dv), lambda b, c: (rows(b, c), vb)),
                  pl.BlockSpec((GROUP, dv), lambda b, c: (rows(b, c), vb + 1)),
                  pl.BlockSpec((GROUP, HEAD), lambda b, c: (rows(b, c), 64)),
                  pl.BlockSpec((1, HEAD), lambda b, c: (0, 0)),
                  pl.BlockSpec((1, dv), lambda b, c: (0, 0)),
                  pl.BlockSpec((None, ML_HEADS, HEAD, ML_DV), lambda b, c: (b, 0, 0, 0)),
                  pl.BlockSpec((None, ML_HEADS, HEAD), lambda b, c: (b, 0, 0)),
                  pl.BlockSpec((None, 1, HEAD), lambda b, c: (b, 0, 0))],
        out_specs=[pl.BlockSpec((GROUP, dv), lambda b, c: (b * nchunks + c, 0)),
                   pl.BlockSpec((None, ML_HEADS, HEAD, ML_DV), lambda b, c: (b, 0, 0, 0)),
                   pl.BlockSpec((None, ML_HEADS, HEAD), lambda b, c: (b, 0, 0)),
                   pl.BlockSpec((None, 1, HEAD), lambda b, c: (b, 0, 0))],
        out_shape=[jax.ShapeDtypeStruct((batch * nchunks * GROUP, dv), BF16),
                   jax.ShapeDtypeStruct((batch, ML_HEADS, HEAD, ML_DV), F32),
                   jax.ShapeDtypeStruct((batch, ML_HEADS, HEAD), F32),
                   jax.ShapeDtypeStruct((batch, 1, HEAD), F32)],
        compiler_params=_cp("parallel", "arbitrary"),
        name="mlstm",
    )(main, main, main, main, tail, bias, nw, c0, n0, m0)


def _merge_kernel(ysb_ref, yml_ref, w0_ref, w1_ref, gsb_ref, gml_ref, o_ref):
    u_sb = jnp.dot(ysb_ref[...], w0_ref[...].astype(BF16), preferred_element_type=F32)
    u_ml = jnp.dot(yml_ref[...], w1_ref[...].astype(BF16), preferred_element_type=F32)
    o_ref[...] = (jax.nn.sigmoid(gsb_ref[...]) * u_sb + jax.nn.sigmoid(gml_ref[...]) * u_ml).astype(o_ref.dtype)


def _merge(y_sb, y_ml, w_branch, tail):
    t, k = y_sb.shape
    d = w_branch.shape[2]
    tm = _pick(t, (1024, 512, 384, 256, 128))
    tn = _pick(d, (512, 256, 128))
    nj = d // tn
    return pl.pallas_call(
        _merge_kernel,
        grid=(t // tm, nj),
        in_specs=[pl.BlockSpec((tm, k), lambda i, j: (i, 0)),
                  pl.BlockSpec((tm, k), lambda i, j: (i, 0)),
                  pl.BlockSpec((None, k, tn), lambda i, j: (0, 0, j)),
                  pl.BlockSpec((None, k, tn), lambda i, j: (1, 0, j)),
                  pl.BlockSpec((tm, tn), lambda i, j: (i, j)),
                  pl.BlockSpec((tm, tn), lambda i, j: (i, nj + j))],
        out_specs=pl.BlockSpec((tm, tn), lambda i, j: (i, j)),
        out_shape=jax.ShapeDtypeStruct((t, d), BF16),
        compiler_params=_cp("parallel", "parallel"),
        name="merge",
    )(y_sb, y_ml, w_branch, w_branch, tail, tail)


def _oproj_kernel(a_ref, w_ref, x_ref, g_ref, o_ref):
    mix = jnp.dot(a_ref[...], w_ref[...].astype(BF16), preferred_element_type=F32)
    tm, tn = mix.shape
    x3 = x_ref[...].reshape(tm // GROUP, GROUP, tn) + g_ref[...] * mix.reshape(tm // GROUP, GROUP, tn)
    o_ref[...] = x3.reshape(tm, tn)


def _oproj(merged, w_o, x, gate):
    t, k = merged.shape
    d = w_o.shape[1]
    tm = _pick(t, (1024, 512, 384, 256, 128))
    tn = _pick(d, (512, 256, 128))
    return pl.pallas_call(
        _oproj_kernel,
        grid=(t // tm, d // tn),
        in_specs=[pl.BlockSpec((tm, k), lambda i, j: (i, 0)),
                  pl.BlockSpec((k, tn), lambda i, j: (0, j)),
                  pl.BlockSpec((tm, tn), lambda i, j: (i, j)),
                  pl.BlockSpec((tm // GROUP, 1, tn), lambda i, j: (i, 0, j))],
        out_specs=pl.BlockSpec((tm, tn), lambda i, j: (i, j)),
        out_shape=jax.ShapeDtypeStruct((t, d), F32),
        compiler_params=_cp("parallel", "parallel"),
        name="oproj",
    )(merged, w_o, x, gate)


def _pack_halves(h):
    d = h.shape[1]
    lo = pltpu.bitcast(h[:, :d // 2].astype(BF16).astype(F32), U32)
    hi = pltpu.bitcast(h[:, d // 2:].astype(BF16).astype(F32), U32)
    return (hi & jnp.uint32(0xFFFF0000)) | (lo >> 16)


def _unpack_halves(w):
    lo = pltpu.bitcast(w << 16, F32).astype(BF16)
    hi = pltpu.bitcast(w & jnp.uint32(0xFFFF0000), F32).astype(BF16)
    return lo, hi


def _modnorm2_kernel(x_ref, nw_ref, sc_ref, sh_ref, whi_ref, wlo_ref, hp_ref, lg_ref):
    h = _modulated(x_ref[...], nw_ref[...], sc_ref[...], sh_ref[...])
    hp_ref[...] = _pack_halves(h)
    h_hi = h.astype(BF16)
    h_lo = (h - h_hi.astype(F32)).astype(BF16)
    whi = whi_ref[...]
    lg_ref[...] = (jnp.dot(h_hi, whi, preferred_element_type=F32)
                   + jnp.dot(h_hi, wlo_ref[...], preferred_element_type=F32)
                   + jnp.dot(h_lo, whi, preferred_element_type=F32))


def _modnorm2(x, nw, sc, sh, w_router):
    t, d = x.shape
    e = w_router.shape[1]
    tm = _pick(t, (256, 128, 64))
    g = tm // GROUP
    whi = w_router.astype(BF16)
    wlo = (w_router - whi.astype(F32)).astype(BF16)
    return pl.pallas_call(
        _modnorm2_kernel,
        grid=(t // tm,),
        in_specs=[pl.BlockSpec((tm, d), lambda i: (i, 0)),
                  pl.BlockSpec((1, d), lambda i: (0, 0)),
                  pl.BlockSpec((g, 1, d), lambda i: (i, 0, 0)),
                  pl.BlockSpec((g, 1, d), lambda i: (i, 0, 0)),
                  pl.BlockSpec((d, e), lambda i: (0, 0)),
                  pl.BlockSpec((d, e), lambda i: (0, 0))],
        out_specs=[pl.BlockSpec((tm, d // 2), lambda i: (i, 0)),
                   pl.BlockSpec((tm, e), lambda i: (i, 0))],
        out_shape=[jax.ShapeDtypeStruct((t, d // 2), U32),
                   jax.ShapeDtypeStruct((t, e), F32)],
        compiler_params=_cp("parallel"),
        name="modnorm2_router",
    )(x, nw.reshape(1, d), sc, sh, whi, wlo)


def _route_kernel(lg_ref, bias_ref, idx_ref, w_ref):
    s = jax.nn.sigmoid(lg_ref[...])
    tm, e = s.shape
    per = e // N_GROUPS
    sb = s + bias_ref[...]
    lane = lax.broadcasted_iota(I32, (tm, e), 1)
    grp = lane // per
    neg = -jnp.inf

    gscore = []
    for g in range(N_GROUPS):
        vals = jnp.where(grp == g, sb, neg)
        m1 = jnp.max(vals, axis=-1, keepdims=True)
        first = jnp.min(jnp.where(vals == m1, lane, e), axis=-1, keepdims=True)
        m2 = jnp.max(jnp.where(lane == first, neg, vals), axis=-1, keepdims=True)
        gscore.append(m1 + m2)

    emask = jnp.zeros((tm, e), jnp.bool_)
    for g in range(N_GROUPS):
        rank = jnp.zeros((tm, 1), I32)
        for o in range(N_GROUPS):
            if o == g:
                continue
            ahead = (gscore[o] > gscore[g]) | ((gscore[o] == gscore[g]) if o < g else False)
            rank = rank + jnp.where(ahead, 1, 0)
        emask = emask | ((grp == g) & (rank < TOPK_GROUPS))

    vals = jnp.where(emask, sb, neg)
    out_lane = lax.broadcasted_iota(I32, (tm, TOP_K), 1)
    idx_out = jnp.zeros((tm, TOP_K), I32)
    w_out = jnp.zeros((tm, TOP_K), F32)
    for kk in range(TOP_K):
        mx = jnp.max(vals, axis=-1, keepdims=True)
        first = jnp.min(jnp.where(vals == mx, lane, e), axis=-1, keepdims=True)
        hit = lane == first
        wk = jnp.sum(jnp.where(hit, s, 0.0), axis=-1, keepdims=True)
        vals = jnp.where(hit, neg, vals)
        idx_out = jnp.where(out_lane == kk, first, idx_out)
        w_out = jnp.where(out_lane == kk, wk, w_out)
    idx_ref[...] = idx_out
    w_ref[...] = w_out / jnp.sum(w_out, axis=-1, keepdims=True) * ROUTED_SCALE


def _route(logits, router_bias):
    t, e = logits.shape
    tm = _pick(t, (256, 128, 64))
    return pl.pallas_call(
        _route_kernel,
        grid=(t // tm,),
        in_specs=[pl.BlockSpec((tm, e), lambda i: (i, 0)),
                  pl.BlockSpec((1, e), lambda i: (0, 0))],
        out_specs=[pl.BlockSpec((tm, TOP_K), lambda i: (i, 0)),
                   pl.BlockSpec((tm, TOP_K), lambda i: (i, 0))],
        out_shape=[jax.ShapeDtypeStruct((t, TOP_K), I32),
                   jax.ShapeDtypeStruct((t, TOP_K), F32)],
        compiler_params=_cp("parallel"),
        name="route",
    )(logits, router_bias.reshape(1, e))


def _dispatch_kernel(tok_ref, x_hbm, o_ref, sem):
    n = o_ref.shape[0]

    def copy(r):
        return pltpu.make_async_copy(x_hbm.at[pl.ds(tok_ref[0, 0, r], 1)], o_ref.at[pl.ds(r, 1)], sem)

    def start(r, _):
        copy(r).start()
        return 0

    def wait(r, _):
        copy(r).wait()
        return 0

    lax.fori_loop(0, n, start, 0, unroll=8)
    lax.fori_loop(0, n, wait, 0, unroll=8)


def _dispatch(hp, tok_sorted):
    t, w = hp.shape
    a = tok_sorted.shape[0]
    n = _pick(a, (GATHER_ROWS, 256, 128))
    return pl.pallas_call(
        _dispatch_kernel,
        grid=(a // n,),
        in_specs=[pl.BlockSpec((1, 1, n), lambda i: (i, 0, 0), memory_space=pltpu.SMEM),
                  pl.BlockSpec(memory_space=pl.ANY)],
        out_specs=pl.BlockSpec((n, w), lambda i: (i, 0)),
        out_shape=jax.ShapeDtypeStruct((a, w), U32),
        scratch_shapes=[pltpu.SemaphoreType.DMA(())],
        compiler_params=_cp("arbitrary"),
        name="dispatch",
    )(tok_sorted.reshape(a // n, 1, n), hp)


def _pair_rows(lo_ref, hi_ref, shape):
    i = pl.program_id(0)
    rows = lax.broadcasted_iota(I32, shape, 0)
    return (rows >= lo_ref[i]) & (rows < hi_ref[i])


def _stage_expert_weights(pe_ref, fe_ref, nxt_ref, w_hbm, w_f32, w_bf, sems):
    i = pl.program_id(0)

    def copies(e):
        return [pltpu.make_async_copy(src.at[e], dst, sems.at[n])
                for n, (src, dst) in enumerate(zip(w_hbm, w_f32))]

    @pl.when(i == 0)
    def _():
        for c in copies(pe_ref[0]):
            c.start()

    @pl.when(fe_ref[i] == 1)
    def _():
        for c in copies(pe_ref[i]):
            c.wait()
        for src, dst in zip(w_f32, w_bf):
            chunk = CAST_CHUNK_ELEMS // src.shape[1]

            def cast(c, _, src=src, dst=dst, chunk=chunk):
                r = pl.ds(pl.multiple_of(c * chunk, chunk), chunk)
                dst[r, :] = src[r, :].astype(BF16)
                return 0

            lax.fori_loop(0, src.shape[0] // chunk, cast, 0)

        @pl.when(nxt_ref[i] >= 0)
        def _():
            for c in copies(nxt_ref[i]):
                c.start()


def _expert_kernel(pe_ref, pb_ref, lo_ref, hi_ref, fe_ref, fb_ref, nxt_ref, x_ref, wg_hbm, wu_hbm, wd_hbm, o_ref,
                   wg_f32, wu_f32, wd_f32, wg_s, wu_s, wd_s, sems):
    i = pl.program_id(0)
    _stage_expert_weights(pe_ref, fe_ref, nxt_ref, (wg_hbm, wu_hbm, wd_hbm), (wg_f32, wu_f32, wd_f32),
                          (wg_s, wu_s, wd_s), sems)

    @pl.when(hi_ref[i] > lo_ref[i])
    def _():
        lo, hi = _unpack_halves(x_ref[...])
        half = lo.shape[1]
        gt = (jnp.dot(lo, wg_s[:half, :], preferred_element_type=F32)
              + jnp.dot(hi, wg_s[half:, :], preferred_element_type=F32))
        up = (jnp.dot(lo, wu_s[:half, :], preferred_element_type=F32)
              + jnp.dot(hi, wu_s[half:, :], preferred_element_type=F32))
        hv = (gt * jax.nn.sigmoid(gt) * up).astype(BF16)
        ov = jnp.dot(hv, wd_s[...], preferred_element_type=F32)
        mine = _pair_rows(lo_ref, hi_ref, ov.shape)

        @pl.when(fb_ref[i] == 1)
        def _():
            o_ref[...] = jnp.where(mine, ov, 0.0)

        @pl.when(fb_ref[i] == 0)
        def _():
            o_ref[...] = jnp.where(mine, ov, o_ref[...])


def _experts(xs, pairs, w_gate, w_up, w_down):
    a, half = xs.shape
    ne, d, de = w_gate.shape
    npairs = pairs[0].shape[0]

    def rows(i, pe, pb, *_):
        return (pb[i], 0)

    hbm = pl.BlockSpec(memory_space=pl.ANY)
    return pl.pallas_call(
        _expert_kernel,
        grid_spec=pltpu.PrefetchScalarGridSpec(
            num_scalar_prefetch=len(pairs),
            grid=(npairs,),
            in_specs=[pl.BlockSpec((MOE_BLOCK, half), rows), hbm, hbm, hbm],
            out_specs=pl.BlockSpec((MOE_BLOCK, d), rows),
            scratch_shapes=[pltpu.VMEM((d, de), F32), pltpu.VMEM((d, de), F32), pltpu.VMEM((de, d), F32),
                            pltpu.VMEM((d, de), BF16), pltpu.VMEM((d, de), BF16), pltpu.VMEM((de, d), BF16),
                            pltpu.SemaphoreType.DMA((3,))]),
        out_shape=jax.ShapeDtypeStruct((a, d), F32),
        compiler_params=_cp("arbitrary"),
        name="experts",
    )(*pairs, xs, w_gate, w_up, w_down)


def _shared_kernel(x_ref, wg_ref, wu_ref, wd_ref, o_ref):
    lo, hi = _unpack_halves(x_ref[...])
    half = lo.shape[1]
    gt = (jnp.dot(lo, wg_ref[:half, :], preferred_element_type=F32)
          + jnp.dot(hi, wg_ref[half:, :], preferred_element_type=F32))
    up = (jnp.dot(lo, wu_ref[:half, :], preferred_element_type=F32)
          + jnp.dot(hi, wu_ref[half:, :], preferred_element_type=F32))
    hh = (gt * jax.nn.sigmoid(gt) * up).astype(BF16)
    o_ref[...] = jnp.dot(hh, wd_ref[...], preferred_element_type=F32)


def _shared(hp, w_gate, w_up, w_down):
    t, half = hp.shape
    d, de = w_gate.shape
    tm = _pick(t, (256, 128, 64))
    return pl.pallas_call(
        _shared_kernel,
        grid=(t // tm,),
        in_specs=[pl.BlockSpec((tm, half), lambda i: (i, 0)),
                  pl.BlockSpec((d, de), lambda i: (0, 0)),
                  pl.BlockSpec((d, de), lambda i: (0, 0)),
                  pl.BlockSpec((de, d), lambda i: (0, 0))],
        out_specs=pl.BlockSpec((tm, d), lambda i: (i, 0)),
        out_shape=jax.ShapeDtypeStruct((t, d), F32),
        compiler_params=_cp("parallel"),
        name="shared_expert",
    )(hp, w_gate.astype(BF16), w_up.astype(BF16), w_down.astype(BF16))


def _combine_kernel(dest_ref, w_ref, sh_ref, x_ref, g_ref, nw_ref, y_hbm, o_ref, buf, sem):
    tt = x_ref.shape[0]

    def copy(t, kk):
        return pltpu.make_async_copy(y_hbm.at[pl.ds(dest_ref[0, 0, t * TOP_K + kk], 1)],
                                     buf.at[kk, pl.ds(t, 1)], sem)

    def start(t, _):
        for kk in range(TOP_K):
            copy(t, kk).start()
        return 0

    def wait(t, _):
        for kk in range(TOP_K):
            copy(t, kk).wait()
        return 0

    lax.fori_loop(0, tt, start, 0)
    lax.fori_loop(0, tt, wait, 0)

    w = w_ref[...]
    ff = sh_ref[...]
    for kk in range(TOP_K):
        ff = ff + w[:, kk:kk + 1] * buf[kk]
    x = x_ref[...] + g_ref[0] * ff
    o_ref[...] = x * lax.rsqrt(jnp.mean(x * x, axis=-1, keepdims=True) + EPS) * nw_ref[...]


def _combine(y_sorted, dest, gw, shared, x, gate, nw):
    t, d = x.shape
    tt = GROUP
    return pl.pallas_call(
        _combine_kernel,
        grid=(t // tt,),
        in_specs=[pl.BlockSpec((1, 1, tt * TOP_K), lambda i: (i, 0, 0), memory_space=pltpu.SMEM),
                  pl.BlockSpec((tt, TOP_K), lambda i: (i, 0)),
                  pl.BlockSpec((tt, d), lambda i: (i, 0)),
                  pl.BlockSpec((tt, d), lambda i: (i, 0)),
                  pl.BlockSpec((1, 1, d), lambda i: (i, 0, 0)),
                  pl.BlockSpec((1, d), lambda i: (0, 0)),
                  pl.BlockSpec(memory_space=pl.ANY)],
        out_specs=pl.BlockSpec((tt, d), lambda i: (i, 0)),
        out_shape=jax.ShapeDtypeStruct((t, d), F32),
        scratch_shapes=[pltpu.VMEM((TOP_K, tt, d), F32), pltpu.SemaphoreType.DMA(())],
        compiler_params=_cp("arbitrary"),
        name="combine_final",
    )(dest.reshape(t // tt, 1, tt * TOP_K), gw, shared, x, gate, nw.reshape(1, d), y_sorted)


def _dispatch_plan(eidx, n_experts):
    t = eidx.shape[0]
    a = t * TOP_K
    assert a % MOE_BLOCK == 0
    flat_e = eidx.reshape(-1)
    order = jnp.argsort(flat_e).astype(I32)
    dest = jnp.argsort(order).astype(I32)
    tok_sorted = order // TOP_K
    counts = jnp.zeros((n_experts,), I32).at[flat_e].add(1)
    cend = jnp.cumsum(counts)
    cstart = cend - counts
    first_blk = cstart // MOE_BLOCK
    nblk = jnp.where(counts > 0, (cend - 1) // MOE_BLOCK - first_blk + 1, 0)
    pend = jnp.cumsum(nblk)
    pstart = pend - nblk
    npairs = a // MOE_BLOCK + n_experts
    p = jnp.arange(npairs, dtype=I32)
    used = p < pend[-1]
    pc = jnp.minimum(p, pend[-1] - 1)
    pe = jnp.minimum(jnp.searchsorted(pend, pc, side='right'), n_experts - 1).astype(I32)
    pb = first_blk[pe] + pc - pstart[pe]
    lo = jnp.maximum(cstart[pe] - pb * MOE_BLOCK, 0)
    hi = jnp.where(used, jnp.minimum(cend[pe] - pb * MOE_BLOCK, MOE_BLOCK), lo)
    prev = lambda v: jnp.concatenate([jnp.full((1,), -1, I32), v[:-1]])
    fe = jnp.where(used & (pe != prev(pe)), 1, 0)
    fb = jnp.where(used & (pb != prev(pb)), 1, 0)
    ids = jnp.arange(n_experts, dtype=I32)
    at_or_after = lax.cummin(jnp.where(counts > 0, ids, n_experts), reverse=True)
    after = jnp.concatenate([at_or_after[1:], jnp.full((1,), n_experts, I32)])
    nxt = jnp.where(after < n_experts, after, -1)[pe]
    return dest, tok_sorted, tuple(v.astype(I32) for v in (pe, pb, lo, hi, fe, fb, nxt))


def kernel(x_prompt, x_sample, cache_sb_k, cache_sb_v, state_mlstm_c, state_mlstm_n, state_mlstm_m, c_prompt, c_sample, w_ada, b_ada, norm1_w, w_in, b_ig, b_fg, ml_norm_w, w_branch, w_o, norm2_w, w_router, router_bias, w_e_gate, w_e_up, w_e_down, w_s_gate, w_s_up, w_s_down, norm_f_w):
    bp, sp, d = x_prompt.shape
    bs, ss, _ = x_sample.shape
    depth = w_ada.shape[0]
    assert depth == 1 and ss == GROUP and sp % SB_TQ == 0
    tp, ts = bp * sp, bs * ss
    t = tp + ts
    d_sb = SB_HEADS * HEAD
    d_mlk = ML_HEADS * HEAD
    d_ml = ML_HEADS * ML_DV
    n_main = 3 * d_sb + 2 * d_mlk + 2 * d_ml
    n_experts = w_router.shape[2]
    l = 0

    x = jnp.concatenate([x_prompt.reshape(tp, d), x_sample.reshape(ts, d)], axis=0)

    c_all = jnp.concatenate([c_prompt, c_sample, jnp.zeros((-(bp + bs) % 8, d), F32)], axis=0)
    mod = _ada(c_all, w_ada[l], b_ada[l])
    mod_g = jnp.concatenate([jnp.repeat(mod[:bp], sp // GROUP, axis=0), mod[bp:bp + bs]], axis=0)
    sh1, sc1, g1, sh2, sc2, g2 = [m.reshape(t // GROUP, 1, d) for m in jnp.split(mod_g, 6, axis=-1)]

    hm = _modnorm(x, norm1_w[l], sc1, sh1)
    qkv, main = _in_proj(hm, w_in[l][:, :n_main].astype(BF16), d_sb, 3 * d_sb)
    w_tail = jnp.concatenate([w_in[l][:, n_main + 2 * ML_HEADS:], w_in[l][:, n_main:n_main + 2 * ML_HEADS],
                              jnp.zeros((d, HEAD - 2 * ML_HEADS), F32)], axis=1).astype(BF16)
    tail = _matmul(hm, w_tail, w_tail.shape[1], "in_proj_tail")

    y_sb_p = _sb_prompt(qkv, bp, sp)
    y_sb_s = _sb_sample(qkv, cache_sb_k[l], cache_sb_v[l], tp // GROUP, bs, ss)
    y_sb = jnp.concatenate([y_sb_p, y_sb_s], axis=0)

    gate_bias = jnp.concatenate([b_ig[l], b_fg[l], jnp.zeros((HEAD - 2 * ML_HEADS,), F32)]).reshape(1, HEAD)
    nw_ml = ml_norm_w[l].reshape(1, d_ml)
    y_ml_p, c_p, n_p, m_p = _mlstm(main, 2 * d_sb, tail, gate_bias, nw_ml,
                                   jnp.zeros((bp, ML_HEADS, HEAD, ML_DV), F32),
                                   jnp.zeros((bp, ML_HEADS, HEAD), F32),
                                   jnp.zeros((bp, 1, HEAD), F32), 0, bp, sp // GROUP)
    m0_s = jnp.pad(state_mlstm_m[l], ((0, 0), (0, HEAD - ML_HEADS))).reshape(bs, 1, HEAD)
    y_ml_s, c_s, n_s, m_s = _mlstm(main, 2 * d_sb, tail, gate_bias, nw_ml, state_mlstm_c[l], state_mlstm_n[l], m0_s,
                                   tp // GROUP, bs, 1)
    y_ml = jnp.concatenate([y_ml_p, y_ml_s], axis=0)

    merged = _merge(y_sb, y_ml, w_branch[l], tail)
    x1 = _oproj(merged, w_o[l], x, g1)

    hp, logits = _modnorm2(x1, norm2_w[l], sc2, sh2, w_router[l])
    eidx, gw = _route(logits, router_bias[l])
    dest, tok_sorted, pairs = _dispatch_plan(eidx, n_experts)
    xs = _dispatch(hp, tok_sorted)
    y_sorted = _experts(xs, pairs, w_e_gate[l], w_e_up[l], w_e_down[l])
    shared = _shared(hp, w_s_gate[l], w_s_up[l], w_s_down[l])
    y = _combine(y_sorted, dest, gw, shared, x1, g2, norm_f_w)

    k_sb = main[:, :d_sb]
    v_sb = main[:, d_sb:2 * d_sb]
    return (y[:tp].reshape(bp, sp, d), y[tp:].reshape(bs, ss, d),
            k_sb[:tp].reshape(1, bp, sp, SB_HEADS, HEAD), v_sb[:tp].reshape(1, bp, sp, SB_HEADS, HEAD),
            c_p[None], n_p[None], m_p[:, 0, :ML_HEADS][None],
            k_sb[tp:].reshape(1, bs, ss, SB_HEADS, HEAD), v_sb[tp:].reshape(1, bs, ss, SB_HEADS, HEAD),
            c_s[None], n_s[None], m_s[:, 0, :ML_HEADS][None])
```

```python
from functools import partial

import jax
import jax.numpy as jnp
from jax import lax
from jax.experimental import pallas as pl
from jax.experimental.pallas import tpu as pltpu

F32 = jnp.float32
BF16 = jnp.bfloat16
U32 = jnp.uint32
I32 = jnp.int32

EPS = 1e-6
GROUP = 64
HEAD = 128
SB_HEADS = 16
ML_HEADS = 8
ML_DV = 256
N_GROUPS = 8
TOPK_GROUPS = 4
TOP_K = 8
ROUTED_SCALE = 2.5
MOE_BLOCK = 128
CAST_CHUNK_ELEMS = 64 * 1024
WEIGHT_DMA_CHUNKS = 8
SB_TQ = 512
SB_TK = 128
SB_TKB = 512
SB_CACHE_BLOCK = SB_TKB
SB_SAMPLE_UNROLL = 8
GATHER_ROWS = 512
VMEM_LIMIT = 56 * 1024 * 1024


def _cp(*sem):
    return pltpu.CompilerParams(dimension_semantics=sem, vmem_limit_bytes=VMEM_LIMIT)


def _pick(n, prefs):
    for p in prefs:
        if n % p == 0:
            return p
    return n


def _ada_kernel(c_ref, w_ref, b_ref, o_ref):
    c = c_ref[...]
    a = (c * jax.nn.sigmoid(c)).astype(BF16)
    o_ref[...] = jnp.dot(a, w_ref[...].astype(BF16), preferred_element_type=F32) + b_ref[...]


def _ada(c, w, b):
    m, k = c.shape
    n = w.shape[1]
    tn = _pick(n, (512, 256, 128))
    return pl.pallas_call(
        _ada_kernel,
        grid=(n // tn,),
        in_specs=[pl.BlockSpec((m, k), lambda j: (0, 0)),
                  pl.BlockSpec((k, tn), lambda j: (0, j)),
                  pl.BlockSpec((1, tn), lambda j: (0, j))],
        out_specs=pl.BlockSpec((m, tn), lambda j: (0, j)),
        out_shape=jax.ShapeDtypeStruct((m, n), F32),
        compiler_params=_cp("parallel"),
        name="ada",
    )(c, w, b.reshape(1, n))


def _modulated(x, nw, sc, sh):
    tm, d = x.shape
    ms = jnp.mean(x * x, axis=-1, keepdims=True)
    y = x * lax.rsqrt(ms + EPS) * nw
    y3 = y.reshape(tm // GROUP, GROUP, d) * (1.0 + sc) + sh
    return y3.reshape(tm, d)


def _modnorm_kernel(x_ref, nw_ref, sc_ref, sh_ref, o_ref):
    o_ref[...] = _modulated(x_ref[...], nw_ref[...], sc_ref[...], sh_ref[...]).astype(o_ref.dtype)


def _modnorm(x, nw, sc, sh):
    t, d = x.shape
    tm = _pick(t, (256, 128, 64))
    g = tm // GROUP
    return pl.pallas_call(
        _modnorm_kernel,
        grid=(t // tm,),
        in_specs=[pl.BlockSpec((tm, d), lambda i: (i, 0)),
                  pl.BlockSpec((1, d), lambda i: (0, 0)),
                  pl.BlockSpec((g, 1, d), lambda i: (i, 0, 0)),
                  pl.BlockSpec((g, 1, d), lambda i: (i, 0, 0))],
        out_specs=pl.BlockSpec((tm, d), lambda i: (i, 0)),
        out_shape=jax.ShapeDtypeStruct((t, d), BF16),
        compiler_params=_cp("parallel"),
        name="modnorm1",
    )(x, nw.reshape(1, d), sc, sh)


def _mm_kernel(a_ref, w_ref, o_ref):
    o_ref[...] = jnp.dot(a_ref[...], w_ref[...].astype(BF16),
                         preferred_element_type=F32).astype(o_ref.dtype)


def _matmul(a, w, n, name):
    m, k = a.shape
    tm = _pick(m, (1024, 512, 384, 256, 128))
    tn = _pick(n, (512, 640, 256, 128))
    return pl.pallas_call(
        _mm_kernel,
        grid=(m // tm, n // tn),
        in_specs=[pl.BlockSpec((tm, k), lambda i, j: (i, 0)),
                  pl.BlockSpec((k, tn), lambda i, j: (0, j))],
        out_specs=pl.BlockSpec((tm, tn), lambda i, j: (i, j)),
        out_shape=jax.ShapeDtypeStruct((m, n), F32),
        compiler_params=_cp("parallel", "parallel"),
        name=name,
    )(a, w)


IN_TN = 512


def _in_proj_kernel(a_ref, w_ref, qkv_ref, f32_ref, *, n_q_tiles, n_sb_tiles):
    j = pl.program_id(1)
    res = jnp.dot(a_ref[...], w_ref[...], preferred_element_type=F32)

    @pl.when(j < n_sb_tiles)
    def _():
        qkv_ref[...] = res.astype(qkv_ref.dtype)

    @pl.when(j >= n_q_tiles)
    def _():
        f32_ref[...] = res


def _in_proj(a, w, n_q, n_sb):
    m, k = a.shape
    n = w.shape[1]
    tm = _pick(m, (1024, 512, 384, 256, 128))
    tn = IN_TN
    nq, nsb = n_q // tn, n_sb // tn
    return pl.pallas_call(
        partial(_in_proj_kernel, n_q_tiles=nq, n_sb_tiles=nsb),
        grid=(m // tm, n // tn),
        in_specs=[pl.BlockSpec((tm, k), lambda i, j: (i, 0)),
                  pl.BlockSpec((k, tn), lambda i, j: (0, j))],
        out_specs=[pl.BlockSpec((tm, tn), lambda i, j: (i, jnp.minimum(j, nsb - 1))),
                   pl.BlockSpec((tm, tn), lambda i, j: (i, jnp.maximum(j - nq, 0)))],
        out_shape=[jax.ShapeDtypeStruct((m, n_sb), BF16),
                   jax.ShapeDtypeStruct((m, n - n_q), F32)],
        compiler_params=_cp("parallel", "arbitrary"),
        name="in_proj_main",
    )(a, w)


def _softplus(z):
    return jnp.maximum(z, 0.0) + jnp.log(1.0 + jnp.exp(-jnp.abs(z)))


def _sb_step(q, k, v, carry, acc, tri, mask):
    z = lax.dot_general(q, k, (((1,), (1,)), ((), ())), preferred_element_type=F32) * (HEAD ** -0.5)
    sp = _softplus(z)
    spm = sp if mask is None else jnp.where(mask, sp, 0.0)
    hi = spm.astype(BF16)
    mid = (spm - hi.astype(F32)).astype(BF16)
    excl = (jnp.dot(hi, tri, preferred_element_type=F32)
            + jnp.dot(mid, tri, preferred_element_type=F32) + carry)
    a = jnp.exp(z - sp - excl)
    if mask is not None:
        a = jnp.where(mask, a, 0.0)
    acc = acc + jnp.dot(a.astype(BF16), v, preferred_element_type=F32)
    carry = carry + jnp.sum(spm, axis=1, keepdims=True)
    return carry, acc


def _tri(n):
    r = lax.broadcasted_iota(I32, (n, n), 0)
    c = lax.broadcasted_iota(I32, (n, n), 1)
    return jnp.where(r > c, 1.0, 0.0).astype(BF16)


def _tri2():
    t = _tri(SB_TK)
    return jnp.concatenate([t, t], axis=0)


def _sb_wide_step(q, k, v, carry, acc, tri2, mask):
    if not isinstance(q, (list, tuple)):
        q, k, v = [q], [k], [v]
    rows = q[0].shape[0]
    tq = rows * len(q)
    nsub = k[0].shape[0] // SB_TK
    z = jnp.concatenate([lax.dot_general(qh, kh, (((1,), (1,)), ((), ())), preferred_element_type=F32)
                         for qh, kh in zip(q, k)], axis=0) * (HEAD ** -0.5)
    sp = _softplus(z)
    spm = sp if mask is None else jnp.where(mask, sp, 0.0)
    hi = spm.astype(BF16)
    mid = (spm - hi.astype(F32)).astype(BF16)
    cols = [slice(j * SB_TK, (j + 1) * SB_TK) for j in range(nsub)]
    lhs = jnp.concatenate([jnp.concatenate([hi[:, c], mid[:, c]], axis=1) for c in cols], axis=0)
    within = jnp.dot(lhs, tri2, preferred_element_type=F32)
    excl = [None] * nsub
    off = carry
    for j in range(nsub - 1, -1, -1):
        excl[j] = within[j * tq:(j + 1) * tq] + off
        off = off + jnp.sum(spm[:, cols[j]], axis=1, keepdims=True)
    a = jnp.exp(z - sp - jnp.concatenate(excl, axis=1))
    if mask is not None:
        a = jnp.where(mask, a, 0.0)
    ab = a.astype(BF16)
    acc = acc + jnp.concatenate([jnp.dot(ab[g * rows:(g + 1) * rows], vh, preferred_element_type=F32)
                                 for g, vh in enumerate(v)], axis=0)
    return off, acc


def _sb_prompt_kernel(q_ref, k_ref, v_ref, o_ref):
    qi = pl.program_id(2)
    tq, tkb = SB_TQ, SB_TKB
    q = q_ref[...]
    tri2 = _tri2()

    def load(kb):
        rows = pl.ds(pl.multiple_of(kb * tkb, tkb), tkb)
        return k_ref[rows, :], v_ref[rows, :]

    k, v = load(qi)
    causal = lax.broadcasted_iota(I32, (tq, tkb), 1) < lax.broadcasted_iota(I32, (tq, tkb), 0)
    carry, acc = _sb_wide_step(q, k, v, jnp.zeros((tq, 1), F32), jnp.zeros((tq, HEAD), F32), tri2, causal)

    def body(i, ca):
        k, v = load(qi - 1 - i)
        return _sb_wide_step(q, k, v, ca[0], ca[1], tri2, None)

    carry, acc = lax.fori_loop(0, qi, body, (carry, acc))
    o_ref[...] = acc.astype(o_ref.dtype)


def _sb_prompt(qkv, batch, seq):
    assert SB_TQ == SB_TKB and seq % SB_TQ == 0
    nq = seq // SB_TQ
    return pl.pallas_call(
        _sb_prompt_kernel,
        grid=(batch, SB_HEADS, nq),
        in_specs=[pl.BlockSpec((SB_TQ, HEAD), lambda b, h, i: (b * nq + i, h)),
                  pl.BlockSpec((seq, HEAD), lambda b, h, i: (b, SB_HEADS + h)),
                  pl.BlockSpec((seq, HEAD), lambda b, h, i: (b, 2 * SB_HEADS + h))],
        out_specs=pl.BlockSpec((SB_TQ, HEAD), lambda b, h, i: (b * nq + i, h)),
        out_shape=jax.ShapeDtypeStruct((batch * seq, SB_HEADS * HEAD), BF16),
        compiler_params=_cp("parallel", "parallel", "arbitrary"),
        name="sb_prompt",
    )(qkv, qkv, qkv)


def _sb_sample_kernel(q_ref, kn_ref, vn_ref, kc_hbm, vc_hbm, o_ref, carry_ref, acc_ref, kbuf, vbuf, sems):
    b = pl.program_id(0)
    s = pl.program_id(1)
    nb = pl.num_programs(0)
    ns = pl.num_programs(1)
    ncb = ns - 1
    lq = q_ref.shape[0]
    cb = kbuf.shape[1]

    def copies(bb, blk, slot):
        rows = pl.ds(pl.multiple_of(blk * cb, cb), cb)
        out = []
        for h in range(SB_HEADS):
            cols = pl.ds(h * HEAD, HEAD)
            out.append(pltpu.make_async_copy(kc_hbm.at[bb, rows, h, :], kbuf.at[slot, :, cols], sems.at[0, slot]))
            out.append(pltpu.make_async_copy(vc_hbm.at[bb, rows, h, :], vbuf.at[slot, :, cols], sems.at[1, slot]))
        return out

    seq = b * ncb + s - 1
    slot = lax.rem(jnp.maximum(seq, 0), 2)

    @pl.when((b == 0) & (s == 0))
    def _():
        for c in copies(0, ncb - 1, 0):
            c.start()

    @pl.when(s > 0)
    def _():
        for c in copies(b, ncb - s, slot):
            c.wait()

        @pl.when(s < ncb)
        def _():
            for c in copies(b, ncb - s - 1, 1 - slot):
                c.start()

        @pl.when((s == ncb) & (b + 1 < nb))
        def _():
            for c in copies(b + 1, ncb - 1, 1 - slot):
                c.start()

    kc_ref = kbuf.at[slot]
    vc_ref = vbuf.at[slot]

    @pl.when(s == 0)
    def _():
        tri = _tri(lq)
        mask = (lax.broadcasted_iota(I32, (lq, lq), 1) < lax.broadcasted_iota(I32, (lq, lq), 0))

        def head(h, _):
            cols = pl.ds(pl.multiple_of(h * HEAD, HEAD), HEAD)
            rows = pl.ds(pl.multiple_of(h * lq, lq), lq)
            carry, acc = _sb_step(q_ref[:, cols], kn_ref[:, cols], vn_ref[:, cols],
                                  jnp.zeros((lq, 1), F32), jnp.zeros((lq, HEAD), F32), tri, mask)
            carry_ref[rows, :] = carry
            acc_ref[rows, :] = acc
            return 0

        lax.fori_loop(0, SB_HEADS, head, 0, unroll=2)

    @pl.when(s > 0)
    def _():
        tri2 = _tri2()
        g = SB_SAMPLE_UNROLL

        def heads(i, _):
            cols = [pl.ds(pl.multiple_of((i * g + u) * HEAD, HEAD), HEAD) for u in range(g)]
            rows = pl.ds(pl.multiple_of(i * (g * lq), g * lq), g * lq)
            carry, acc = _sb_wide_step([q_ref[:, c] for c in cols],
                                       [kc_ref[:, c].astype(BF16) for c in cols],
                                       [vc_ref[:, c].astype(BF16) for c in cols],
                                       carry_ref[rows, :], acc_ref[rows, :], tri2, None)
            carry_ref[rows, :] = carry
            acc_ref[rows, :] = acc
            return 0

        lax.fori_loop(0, SB_HEADS // g, heads, 0)

    @pl.when(s == ns - 1)
    def _():
        for h in range(SB_HEADS):
            o_ref[:, h * HEAD:(h + 1) * HEAD] = acc_ref[h * lq:(h + 1) * lq, :].astype(o_ref.dtype)


def _sb_sample(main, cache_k, cache_v, row_block0, batch, lq):
    past = cache_k.shape[1]
    cb = SB_CACHE_BLOCK
    assert past % cb == 0 and past >= cb
    ncb = past // cb
    dsb = SB_HEADS * HEAD
    hbm = pl.BlockSpec(memory_space=pl.ANY)
    return pl.pallas_call(
        _sb_sample_kernel,
        grid=(batch, ncb + 1),
        in_specs=[pl.BlockSpec((lq, dsb), lambda b, s: (row_block0 + b, 0)),
                  pl.BlockSpec((lq, dsb), lambda b, s: (row_block0 + b, 1)),
                  pl.BlockSpec((lq, dsb), lambda b, s: (row_block0 + b, 2)),
                  hbm, hbm],
        out_specs=pl.BlockSpec((lq, dsb), lambda b, s: (b, 0)),
        out_shape=jax.ShapeDtypeStruct((batch * lq, dsb), BF16),
        scratch_shapes=[pltpu.VMEM((SB_HEADS * lq, 1), F32),
                        pltpu.VMEM((SB_HEADS * lq, HEAD), F32),
                        pltpu.VMEM((2, cb, dsb), F32), pltpu.VMEM((2, cb, dsb), F32),
                        pltpu.SemaphoreType.DMA((2, 2))],
        compiler_params=_cp("arbitrary", "arbitrary"),
        name="sb_sample",
    )(main, main, main, cache_k, cache_v)


def _split3(x):
    p0 = x.astype(BF16)
    r = x - p0.astype(F32)
    p1 = r.astype(BF16)
    p2 = (r - p1.astype(F32)).astype(BF16)
    return p0, p1, p2


def _mlstm_kernel(q_ref, k_ref, v_ref, o_ref, if_ref, bias_ref, nw_ref, c0_ref, n0_ref, m0_ref,
                  y_ref, c_s, n_s, m_s):
    ci = pl.program_id(1)
    L = GROUP

    @pl.when(ci == 0)
    def _():
        c_s[...] = c0_ref[...]
        n_s[...] = n0_ref[...]
        m_s[...] = m0_ref[...]

    g = if_ref[...] + bias_ref[...]
    logf = pltpu.roll(jax.nn.log_sigmoid(g), HEAD - ML_HEADS, axis=1)
    row = lax.broadcasted_iota(I32, (L, L), 0)
    col = lax.broadcasted_iota(I32, (L, L), 1)
    tril = row >= col
    tril_b = jnp.where(tril, 1.0, 0.0).astype(BF16)
    b = sum(jnp.dot(tril_b, p, preferred_element_type=F32) for p in _split3(logf))
    b_t = b.T
    i_t = g.T
    m_vec = m_s[...]
    lane = lax.broadcasted_iota(I32, (1, HEAD), 1)
    m_out = m_vec

    for h in range(ML_HEADS):
        q = q_ref[:, h * HEAD:(h + 1) * HEAD].astype(BF16)
        kf = k_ref[:, h * HEAD:(h + 1) * HEAD] * (HEAD ** -0.5)
        k = kf.astype(BF16)
        vf = v_ref[:, h * ML_DV:(h + 1) * ML_DV]
        v = vf.astype(BF16)
        c = c_s[h]
        n = n_s[h:h + 1, :]
        m = m_vec[:, h:h + 1]
        b_col = b[:, h:h + 1]
        i_col = g[:, h:h + 1]
        b_row = b_t[h:h + 1, :]
        i_row = i_t[h:h + 1, :]

        d_log = jnp.where(tril, b_col - b_row + i_row, -jnp.inf)
        inter = b_col + m
        m_t = jnp.maximum(jnp.max(d_log, axis=-1, keepdims=True), inter)
        qk = lax.dot_general(q, k, (((1,), (1,)), ((), ())), preferred_element_type=F32)
        w_intra = jnp.exp(d_log - m_t) * qk
        w_inter = jnp.exp(inter - m_t)
        qc = jnp.dot(q, c.astype(BF16), preferred_element_type=F32)
        num = jnp.dot(w_intra.astype(BF16), v, preferred_element_type=F32) + w_inter * qc
        qn = jnp.sum(q.astype(F32) * n, axis=-1, keepdims=True)
        den = jnp.sum(w_intra, axis=-1, keepdims=True) + w_inter * qn
        hh = num / jnp.maximum(jnp.abs(den), jnp.exp(-m_t))

        b_last = b_col[L - 1:L, :]
        log_src = b_last - b_col + i_col
        m_new = jnp.maximum(b_last + m, jnp.max(log_src, axis=0, keepdims=True))
        src_w = jnp.exp(log_src - m_new)
        car_w = jnp.exp(b_last + m - m_new)
        c_s[h] = car_w * c + lax.dot_general(k, (src_w * vf).astype(BF16), (((0,), (0,)), ((), ())),
                                             preferred_element_type=F32)
        n_s[h:h + 1, :] = car_w * n + jnp.sum(src_w * k.astype(F32), axis=0, keepdims=True)
        m_out = jnp.where(lane == h, m_new, m_out)

        hn = hh * lax.rsqrt(jnp.mean(hh * hh, axis=-1, keepdims=True) + EPS)
        gate = jax.nn.sigmoid(o_ref[:, h * ML_DV:(h + 1) * ML_DV])
        y_ref[:, h * ML_DV:(h + 1) * ML_DV] = (hn * nw_ref[:, h * ML_DV:(h + 1) * ML_DV] * gate).astype(y_ref.dtype)

    m_s[...] = m_out


def _mlstm(main, col0, tail, bias, nw, c0, n0, m0, row_block0, batch, nchunks):
    dk = ML_HEADS * HEAD
    dv = ML_HEADS * ML_DV
    assert col0 % dv == 0
    qb, vb = col0 // dk, col0 // dv + 1

    def rows(b, c):
        return row_block0 + b * nchunks + c

    return pl.pallas_call(
        _mlstm_kernel,
        grid=(batch, nchunks),
        in_specs=[pl.BlockSpec((GROUP, dk), lambda b, c: (rows(b, c), qb)),
                  pl.BlockSpec((GROUP, dk), lambda b, c: (rows(b, c), qb + 1)),
                  pl.BlockSpec((GROUP, dv), lambda b, c: (rows(b, c), vb)),
                  pl.BlockSpec((GROUP, dv), lambda b, c: (rows(b, c), vb + 1)),
                  pl.BlockSpec((GROUP, HEAD), lambda b, c: (rows(b, c), 64)),
                  pl.BlockSpec((1, HEAD), lambda b, c: (0, 0)),
                  pl.BlockSpec((1, dv), lambda b, c: (0, 0)),
                  pl.BlockSpec((None, ML_HEADS, HEAD, ML_DV), lambda b, c: (b, 0, 0, 0)),
                  pl.BlockSpec((None, ML_HEADS, HEAD), lambda b, c: (b, 0, 0)),
                  pl.BlockSpec((None, 1, HEAD), lambda b, c: (b, 0, 0))],
        out_specs=[pl.BlockSpec((GROUP, dv), lambda b, c: (b * nchunks + c, 0)),
                   pl.BlockSpec((None, ML_HEADS, HEAD, ML_DV), lambda b, c: (b, 0, 0, 0)),
                   pl.BlockSpec((None, ML_HEADS, HEAD), lambda b, c: (b, 0, 0)),
                   pl.BlockSpec((None, 1, HEAD), lambda b, c: (b, 0, 0))],
        out_shape=[jax.ShapeDtypeStruct((batch * nchunks * GROUP, dv), BF16),
                   jax.ShapeDtypeStruct((batch, ML_HEADS, HEAD, ML_DV), F32),
                   jax.ShapeDtypeStruct((batch, ML_HEADS, HEAD), F32),
                   jax.ShapeDtypeStruct((batch, 1, HEAD), F32)],
        compiler_params=_cp("parallel", "arbitrary"),
        name="mlstm",
    )(main, main, main, main, tail, bias, nw, c0, n0, m0)


def _merge_kernel(ysb_ref, yml_ref, w0_ref, w1_ref, gsb_ref, gml_ref, o_ref):
    u_sb = jnp.dot(ysb_ref[...], w0_ref[...].astype(BF16), preferred_element_type=F32)
    u_ml = jnp.dot(yml_ref[...], w1_ref[...].astype(BF16), preferred_element_type=F32)
    o_ref[...] = (jax.nn.sigmoid(gsb_ref[...]) * u_sb + jax.nn.sigmoid(gml_ref[...]) * u_ml).astype(o_ref.dtype)


def _merge(y_sb, y_ml, w_branch, tail):
    t, k = y_sb.shape
    d = w_branch.shape[2]
    tm = _pick(t, (1024, 512, 384, 256, 128))
    tn = _pick(d, (512, 256, 128))
    nj = d // tn
    return pl.pallas_call(
        _merge_kernel,
        grid=(t // tm, nj),
        in_specs=[pl.BlockSpec((tm, k), lambda i, j: (i, 0)),
                  pl.BlockSpec((tm, k), lambda i, j: (i, 0)),
                  pl.BlockSpec((None, k, tn), lambda i, j: (0, 0, j)),
                  pl.BlockSpec((None, k, tn), lambda i, j: (1, 0, j)),
                  pl.BlockSpec((tm, tn), lambda i, j: (i, j)),
                  pl.BlockSpec((tm, tn), lambda i, j: (i, nj + j))],
        out_specs=pl.BlockSpec((tm, tn), lambda i, j: (i, j)),
        out_shape=jax.ShapeDtypeStruct((t, d), BF16),
        compiler_params=_cp("parallel", "parallel"),
        name="merge",
    )(y_sb, y_ml, w_branch, w_branch, tail, tail)


def _oproj_kernel(a_ref, w_ref, x_ref, g_ref, o_ref):
    mix = jnp.dot(a_ref[...], w_ref[...].astype(BF16), preferred_element_type=F32)
    tm, tn = mix.shape
    x3 = x_ref[...].reshape(tm // GROUP, GROUP, tn) + g_ref[...] * mix.reshape(tm // GROUP, GROUP, tn)
    o_ref[...] = x3.reshape(tm, tn)


def _oproj(merged, w_o, x, gate):
    t, k = merged.shape
    d = w_o.shape[1]
    tm = _pick(t, (1024, 512, 384, 256, 128))
    tn = _pick(d, (512, 256, 128))
    return pl.pallas_call(
        _oproj_kernel,
        grid=(t // tm, d // tn),
        in_specs=[pl.BlockSpec((tm, k), lambda i, j: (i, 0)),
                  pl.BlockSpec((k, tn), lambda i, j: (0, j)),
                  pl.BlockSpec((tm, tn), lambda i, j: (i, j)),
                  pl.BlockSpec((tm // GROUP, 1, tn), lambda i, j: (i, 0, j))],
        out_specs=pl.BlockSpec((tm, tn), lambda i, j: (i, j)),
        out_shape=jax.ShapeDtypeStruct((t, d), F32),
        compiler_params=_cp("parallel", "parallel"),
        name="oproj",
    )(merged, w_o, x, gate)


def _pack_halves(h):
    d = h.shape[1]
    lo = pltpu.bitcast(h[:, :d // 2].astype(BF16).astype(F32), U32)
    hi = pltpu.bitcast(h[:, d // 2:].astype(BF16).astype(F32), U32)
    return (hi & jnp.uint32(0xFFFF0000)) | (lo >> 16)


def _unpack_halves(w):
    lo = pltpu.bitcast(w << 16, F32).astype(BF16)
    hi = pltpu.bitcast(w & jnp.uint32(0xFFFF0000), F32).astype(BF16)
    return lo, hi


def _modnorm2_kernel(x_ref, nw_ref, sc_ref, sh_ref, whi_ref, wlo_ref, hp_ref, lg_ref):
    h = _modulated(x_ref[...], nw_ref[...], sc_ref[...], sh_ref[...])
    hp_ref[...] = _pack_halves(h)
    h_hi = h.astype(BF16)
    h_lo = (h - h_hi.astype(F32)).astype(BF16)
    whi = whi_ref[...]
    lg_ref[...] = (jnp.dot(h_hi, whi, preferred_element_type=F32)
                   + jnp.dot(h_hi, wlo_ref[...], preferred_element_type=F32)
                   + jnp.dot(h_lo, whi, preferred_element_type=F32))


def _modnorm2(x, nw, sc, sh, w_router):
    t, d = x.shape
    e = w_router.shape[1]
    tm = _pick(t, (256, 128, 64))
    g = tm // GROUP
    whi = w_router.astype(BF16)
    wlo = (w_router - whi.astype(F32)).astype(BF16)
    return pl.pallas_call(
        _modnorm2_kernel,
        grid=(t // tm,),
        in_specs=[pl.BlockSpec((tm, d), lambda i: (i, 0)),
                  pl.BlockSpec((1, d), lambda i: (0, 0)),
                  pl.BlockSpec((g, 1, d), lambda i: (i, 0, 0)),
                  pl.BlockSpec((g, 1, d), lambda i: (i, 0, 0)),
                  pl.BlockSpec((d, e), lambda i: (0, 0)),
                  pl.BlockSpec((d, e), lambda i: (0, 0))],
        out_specs=[pl.BlockSpec((tm, d // 2), lambda i: (i, 0)),
                   pl.BlockSpec((tm, e), lambda i: (i, 0))],
        out_shape=[jax.ShapeDtypeStruct((t, d // 2), U32),
                   jax.ShapeDtypeStruct((t, e), F32)],
        compiler_params=_cp("parallel"),
        name="modnorm2_router",
    )(x, nw.reshape(1, d), sc, sh, whi, wlo)


def _route_kernel(lg_ref, bias_ref, idx_ref, w_ref):
    s = jax.nn.sigmoid(lg_ref[...])
    tm, e = s.shape
    per = e // N_GROUPS
    sb = s + bias_ref[...]
    lane = lax.broadcasted_iota(I32, (tm, e), 1)
    grp = lane // per
    neg = -jnp.inf

    gscore = []
    for g in range(N_GROUPS):
        vals = jnp.where(grp == g, sb, neg)
        m1 = jnp.max(vals, axis=-1, keepdims=True)
        first = jnp.min(jnp.where(vals == m1, lane, e), axis=-1, keepdims=True)
        m2 = jnp.max(jnp.where(lane == first, neg, vals), axis=-1, keepdims=True)
        gscore.append(m1 + m2)

    emask = jnp.zeros((tm, e), jnp.bool_)
    for g in range(N_GROUPS):
        rank = jnp.zeros((tm, 1), I32)
        for o in range(N_GROUPS):
            if o == g:
                continue
            ahead = (gscore[o] > gscore[g]) | ((gscore[o] == gscore[g]) if o < g else False)
            rank = rank + jnp.where(ahead, 1, 0)
        emask = emask | ((grp == g) & (rank < TOPK_GROUPS))

    vals = jnp.where(emask, sb, neg)
    out_lane = lax.broadcasted_iota(I32, (tm, TOP_K), 1)
    idx_out = jnp.zeros((tm, TOP_K), I32)
    w_out = jnp.zeros((tm, TOP_K), F32)
    for kk in range(TOP_K):
        mx = jnp.max(vals, axis=-1, keepdims=True)
        first = jnp.min(jnp.where(vals == mx, lane, e), axis=-1, keepdims=True)
        hit = lane == first
        wk = jnp.sum(jnp.where(hit, s, 0.0), axis=-1, keepdims=True)
        vals = jnp.where(hit, neg, vals)
        idx_out = jnp.where(out_lane == kk, first, idx_out)
        w_out = jnp.where(out_lane == kk, wk, w_out)
    idx_ref[...] = idx_out
    w_ref[...] = w_out / jnp.sum(w_out, axis=-1, keepdims=True) * ROUTED_SCALE


def _route(logits, router_bias):
    t, e = logits.shape
    tm = _pick(t, (256, 128, 64))
    return pl.pallas_call(
        _route_kernel,
        grid=(t // tm,),
        in_specs=[pl.BlockSpec((tm, e), lambda i: (i, 0)),
                  pl.BlockSpec((1, e), lambda i: (0, 0))],
        out_specs=[pl.BlockSpec((tm, TOP_K), lambda i: (i, 0)),
                   pl.BlockSpec((tm, TOP_K), lambda i: (i, 0))],
        out_shape=[jax.ShapeDtypeStruct((t, TOP_K), I32),
                   jax.ShapeDtypeStruct((t, TOP_K), F32)],
        compiler_params=_cp("parallel"),
        name="route",
    )(logits, router_bias.reshape(1, e))


def _dispatch_kernel(tok_ref, x_hbm, o_ref, sem):
    n = o_ref.shape[0]

    def copy(r):
        return pltpu.make_async_copy(x_hbm.at[pl.ds(tok_ref[0, 0, r], 1)], o_ref.at[pl.ds(r, 1)], sem)

    def start(r, _):
        copy(r).start()
        return 0

    def wait(r, _):
        copy(r).wait()
        return 0

    lax.fori_loop(0, n, start, 0, unroll=8)
    lax.fori_loop(0, n, wait, 0, unroll=8)


def _dispatch(hp, tok_sorted):
    t, w = hp.shape
    a = tok_sorted.shape[0]
    n = _pick(a, (GATHER_ROWS, 256, 128))
    return pl.pallas_call(
        _dispatch_kernel,
        grid=(a // n,),
        in_specs=[pl.BlockSpec((1, 1, n), lambda i: (i, 0, 0), memory_space=pltpu.SMEM),
                  pl.BlockSpec(memory_space=pl.ANY)],
        out_specs=pl.BlockSpec((n, w), lambda i: (i, 0)),
        out_shape=jax.ShapeDtypeStruct((a, w), U32),
        scratch_shapes=[pltpu.SemaphoreType.DMA(())],
        compiler_params=_cp("arbitrary"),
        name="dispatch",
    )(tok_sorted.reshape(a // n, 1, n), hp)


def _pair_rows(lo_ref, hi_ref, shape):
    i = pl.program_id(0)
    rows = lax.broadcasted_iota(I32, shape, 0)
    return (rows >= lo_ref[i]) & (rows < hi_ref[i])


def _stage_expert_weights(pe_ref, fe_ref, nxt_ref, w_hbm, w_f32, w_bf, sems):
    i = pl.program_id(0)

    def copies(e):
        out = []
        for n, (src, dst) in enumerate(zip(w_hbm, w_f32)):
            rows = dst.shape[0] // WEIGHT_DMA_CHUNKS
            for c in range(WEIGHT_DMA_CHUNKS):
                r = pl.ds(c * rows, rows)
                out.append(pltpu.make_async_copy(src.at[e, r], dst.at[r], sems.at[n]))
        return out

    @pl.when(i == 0)
    def _():
        for c in copies(pe_ref[0]):
            c.start()

    @pl.when(fe_ref[i] == 1)
    def _():
        for c in copies(pe_ref[i]):
            c.wait()
        for src, dst in zip(w_f32, w_bf):
            chunk = CAST_CHUNK_ELEMS // src.shape[1]

            def cast(c, _, src=src, dst=dst, chunk=chunk):
                r = pl.ds(pl.multiple_of(c * chunk, chunk), chunk)
                dst[r, :] = src[r, :].astype(BF16)
                return 0

            lax.fori_loop(0, src.shape[0] // chunk, cast, 0)

        @pl.when(nxt_ref[i] >= 0)
        def _():
            for c in copies(nxt_ref[i]):
                c.start()


def _expert_kernel(pe_ref, pb_ref, lo_ref, hi_ref, fe_ref, fb_ref, nxt_ref, x_ref, wg_hbm, wu_hbm, wd_hbm, o_ref,
                   wg_f32, wu_f32, wd_f32, wg_s, wu_s, wd_s, sems):
    i = pl.program_id(0)
    _stage_expert_weights(pe_ref, fe_ref, nxt_ref, (wg_hbm, wu_hbm, wd_hbm), (wg_f32, wu_f32, wd_f32),
                          (wg_s, wu_s, wd_s), sems)

    @pl.when(hi_ref[i] > lo_ref[i])
    def _():
        lo, hi = _unpack_halves(x_ref[...])
        half = lo.shape[1]
        gt = (jnp.dot(lo, wg_s[:half, :], preferred_element_type=F32)
              + jnp.dot(hi, wg_s[half:, :], preferred_element_type=F32))
        up = (jnp.dot(lo, wu_s[:half, :], preferred_element_type=F32)
              + jnp.dot(hi, wu_s[half:, :], preferred_element_type=F32))
        hv = (gt * jax.nn.sigmoid(gt) * up).astype(BF16)
        ov = jnp.dot(hv, wd_s[...], preferred_element_type=F32)
        mine = _pair_rows(lo_ref, hi_ref, ov.shape)

        @pl.when(fb_ref[i] == 1)
        def _():
            o_ref[...] = jnp.where(mine, ov, 0.0)

        @pl.when(fb_ref[i] == 0)
        def _():
            o_ref[...] = jnp.where(mine, ov, o_ref[...])


def _experts(xs, pairs, w_gate, w_up, w_down):
    a, half = xs.shape
    ne, d, de = w_gate.shape
    npairs = pairs[0].shape[0]

    def rows(i, pe, pb, *_):
        return (pb[i], 0)

    hbm = pl.BlockSpec(memory_space=pl.ANY)
    return pl.pallas_call(
        _expert_kernel,
        grid_spec=pltpu.PrefetchScalarGridSpec(
            num_scalar_prefetch=len(pairs),
            grid=(npairs,),
            in_specs=[pl.BlockSpec((MOE_BLOCK, half), rows), hbm, hbm, hbm],
            out_specs=pl.BlockSpec((MOE_BLOCK, d), rows),
            scratch_shapes=[pltpu.VMEM((d, de), F32), pltpu.VMEM((d, de), F32), pltpu.VMEM((de, d), F32),
                            pltpu.VMEM((d, de), BF16), pltpu.VMEM((d, de), BF16), pltpu.VMEM((de, d), BF16),
                            pltpu.SemaphoreType.DMA((3,))]),
        out_shape=jax.ShapeDtypeStruct((a, d), F32),
        compiler_params=_cp("arbitrary"),
        name="experts",
    )(*pairs, xs, w_gate, w_up, w_down)


def _shared_kernel(x_ref, wg_ref, wu_ref, wd_ref, o_ref):
    lo, hi = _unpack_halves(x_ref[...])
    half = lo.shape[1]
    gt = (jnp.dot(lo, wg_ref[:half, :], preferred_element_type=F32)
          + jnp.dot(hi, wg_ref[half:, :], preferred_element_type=F32))
    up = (jnp.dot(lo, wu_ref[:half, :], preferred_element_type=F32)
          + jnp.dot(hi, wu_ref[half:, :], preferred_element_type=F32))
    hh = (gt * jax.nn.sigmoid(gt) * up).astype(BF16)
    o_ref[...] = jnp.dot(hh, wd_ref[...], preferred_element_type=F32)


def _shared(hp, w_gate, w_up, w_down):
    t, half = hp.shape
    d, de = w_gate.shape
    tm = _pick(t, (256, 128, 64))
    return pl.pallas_call(
        _shared_kernel,
        grid=(t // tm,),
        in_specs=[pl.BlockSpec((tm, half), lambda i: (i, 0)),
                  pl.BlockSpec((d, de), lambda i: (0, 0)),
                  pl.BlockSpec((d, de), lambda i: (0, 0)),
                  pl.BlockSpec((de, d), lambda i: (0, 0))],
        out_specs=pl.BlockSpec((tm, d), lambda i: (i, 0)),
        out_shape=jax.ShapeDtypeStruct((t, d), F32),
        compiler_params=_cp("parallel"),
        name="shared_expert",
    )(hp, w_gate.astype(BF16), w_up.astype(BF16), w_down.astype(BF16))


def _combine_kernel(dest_ref, w_ref, sh_ref, x_ref, g_ref, nw_ref, y_hbm, o_ref, buf, sem):
    tt = x_ref.shape[0]

    def copy(t, kk):
        return pltpu.make_async_copy(y_hbm.at[pl.ds(dest_ref[0, 0, t * TOP_K + kk], 1)],
                                     buf.at[kk, pl.ds(t, 1)], sem)

    def start(t, _):
        for kk in range(TOP_K):
            copy(t, kk).start()
        return 0

    def wait(t, _):
        for kk in range(TOP_K):
            copy(t, kk).wait()
        return 0

    lax.fori_loop(0, tt, start, 0)
    lax.fori_loop(0, tt, wait, 0)

    w = w_ref[...]
    ff = sh_ref[...]
    for kk in range(TOP_K):
        ff = ff + w[:, kk:kk + 1] * buf[kk]
    x = x_ref[...] + g_ref[0] * ff
    o_ref[...] = x * lax.rsqrt(jnp.mean(x * x, axis=-1, keepdims=True) + EPS) * nw_ref[...]


def _combine(y_sorted, dest, gw, shared, x, gate, nw):
    t, d = x.shape
    tt = GROUP
    return pl.pallas_call(
        _combine_kernel,
        grid=(t // tt,),
        in_specs=[pl.BlockSpec((1, 1, tt * TOP_K), lambda i: (i, 0, 0), memory_space=pltpu.SMEM),
                  pl.BlockSpec((tt, TOP_K), lambda i: (i, 0)),
                  pl.BlockSpec((tt, d), lambda i: (i, 0)),
                  pl.BlockSpec((tt, d), lambda i: (i, 0)),
                  pl.BlockSpec((1, 1, d), lambda i: (i, 0, 0)),
                  pl.BlockSpec((1, d), lambda i: (0, 0)),
                  pl.BlockSpec(memory_space=pl.ANY)],
        out_specs=pl.BlockSpec((tt, d), lambda i: (i, 0)),
        out_shape=jax.ShapeDtypeStruct((t, d), F32),
        scratch_shapes=[pltpu.VMEM((TOP_K, tt, d), F32), pltpu.SemaphoreType.DMA(())],
        compiler_params=_cp("arbitrary"),
        name="combine_final",
    )(dest.reshape(t // tt, 1, tt * TOP_K), gw, shared, x, gate, nw.reshape(1, d), y_sorted)


def _dispatch_plan(eidx, n_experts):
    t = eidx.shape[0]
    a = t * TOP_K
    assert a % MOE_BLOCK == 0
    flat_e = eidx.reshape(-1)
    order = jnp.argsort(flat_e).astype(I32)
    dest = jnp.argsort(order).astype(I32)
    tok_sorted = order // TOP_K
    counts = jnp.zeros((n_experts,), I32).at[flat_e].add(1)
    cend = jnp.cumsum(counts)
    cstart = cend - counts
    first_blk = cstart // MOE_BLOCK
    nblk = jnp.where(counts > 0, (cend - 1) // MOE_BLOCK - first_blk + 1, 0)
    pend = jnp.cumsum(nblk)
    pstart = pend - nblk
    npairs = a // MOE_BLOCK + n_experts
    p = jnp.arange(npairs, dtype=I32)
    used = p < pend[-1]
    pc = jnp.minimum(p, pend[-1] - 1)
    pe = jnp.minimum(jnp.searchsorted(pend, pc, side='right'), n_experts - 1).astype(I32)
    pb = first_blk[pe] + pc - pstart[pe]
    lo = jnp.maximum(cstart[pe] - pb * MOE_BLOCK, 0)
    hi = jnp.where(used, jnp.minimum(cend[pe] - pb * MOE_BLOCK, MOE_BLOCK), lo)
    prev = lambda v: jnp.concatenate([jnp.full((1,), -1, I32), v[:-1]])
    fe = jnp.where(used & (pe != prev(pe)), 1, 0)
    fb = jnp.where(used & (pb != prev(pb)), 1, 0)
    ids = jnp.arange(n_experts, dtype=I32)
    at_or_after = lax.cummin(jnp.where(counts > 0, ids, n_experts), reverse=True)
    after = jnp.concatenate([at_or_after[1:], jnp.full((1,), n_experts, I32)])
    nxt = jnp.where(after < n_experts, after, -1)[pe]
    return dest, tok_sorted, tuple(v.astype(I32) for v in (pe, pb, lo, hi, fe, fb, nxt))


def kernel(x_prompt, x_sample, cache_sb_k, cache_sb_v, state_mlstm_c, state_mlstm_n, state_mlstm_m, c_prompt, c_sample, w_ada, b_ada, norm1_w, w_in, b_ig, b_fg, ml_norm_w, w_branch, w_o, norm2_w, w_router, router_bias, w_e_gate, w_e_up, w_e_down, w_s_gate, w_s_up, w_s_down, norm_f_w):
    bp, sp, d = x_prompt.shape
    bs, ss, _ = x_sample.shape
    depth = w_ada.shape[0]
    assert depth == 1 and ss == GROUP and sp % SB_TQ == 0
    tp, ts = bp * sp, bs * ss
    t = tp + ts
    d_sb = SB_HEADS * HEAD
    d_mlk = ML_HEADS * HEAD
    d_ml = ML_HEADS * ML_DV
    n_main = 3 * d_sb + 2 * d_mlk + 2 * d_ml
    n_experts = w_router.shape[2]
    l = 0

    x = jnp.concatenate([x_prompt.reshape(tp, d), x_sample.reshape(ts, d)], axis=0)

    c_all = jnp.concatenate([c_prompt, c_sample, jnp.zeros((-(bp + bs) % 8, d), F32)], axis=0)
    mod = _ada(c_all, w_ada[l], b_ada[l])
    mod_g = jnp.concatenate([jnp.repeat(mod[:bp], sp // GROUP, axis=0), mod[bp:bp + bs]], axis=0)
    sh1, sc1, g1, sh2, sc2, g2 = [m.reshape(t // GROUP, 1, d) for m in jnp.split(mod_g, 6, axis=-1)]

    hm = _modnorm(x, norm1_w[l], sc1, sh1)
    qkv, main = _in_proj(hm, w_in[l][:, :n_main].astype(BF16), d_sb, 3 * d_sb)
    w_tail = jnp.concatenate([w_in[l][:, n_main + 2 * ML_HEADS:], w_in[l][:, n_main:n_main + 2 * ML_HEADS],
                              jnp.zeros((d, HEAD - 2 * ML_HEADS), F32)], axis=1).astype(BF16)
    tail = _matmul(hm, w_tail, w_tail.shape[1], "in_proj_tail")

    y_sb_p = _sb_prompt(qkv, bp, sp)
    y_sb_s = _sb_sample(qkv, cache_sb_k[l], cache_sb_v[l], tp // GROUP, bs, ss)
    y_sb = jnp.concatenate([y_sb_p, y_sb_s], axis=0)

    gate_bias = jnp.concatenate([b_ig[l], b_fg[l], jnp.zeros((HEAD - 2 * ML_HEADS,), F32)]).reshape(1, HEAD)
    nw_ml = ml_norm_w[l].reshape(1, d_ml)
    y_ml_p, c_p, n_p, m_p = _mlstm(main, 2 * d_sb, tail, gate_bias, nw_ml,
                                   jnp.zeros((bp, ML_HEADS, HEAD, ML_DV), F32),
                                   jnp.zeros((bp, ML_HEADS, HEAD), F32),
                                   jnp.zeros((bp, 1, HEAD), F32), 0, bp, sp // GROUP)
    m0_s = jnp.pad(state_mlstm_m[l], ((0, 0), (0, HEAD - ML_HEADS))).reshape(bs, 1, HEAD)
    y_ml_s, c_s, n_s, m_s = _mlstm(main, 2 * d_sb, tail, gate_bias, nw_ml, state_mlstm_c[l], state_mlstm_n[l], m0_s,
                                   tp // GROUP, bs, 1)
    y_ml = jnp.concatenate([y_ml_p, y_ml_s], axis=0)

    merged = _merge(y_sb, y_ml, w_branch[l], tail)
    x1 = _oproj(merged, w_o[l], x, g1)

    hp, logits = _modnorm2(x1, norm2_w[l], sc2, sh2, w_router[l])
    eidx, gw = _route(logits, router_bias[l])
    dest, tok_sorted, pairs = _dispatch_plan(eidx, n_experts)
    xs = _dispatch(hp, tok_sorted)
    y_sorted = _experts(xs, pairs, w_e_gate[l], w_e_up[l], w_e_down[l])
    shared = _shared(hp, w_s_gate[l], w_s_up[l], w_s_down[l])
    y = _combine(y_sorted, dest, gw, shared, x1, g2, norm_f_w)

    k_sb = main[:, :d_sb]
    v_sb = main[:, d_sb:2 * d_sb]
    return (y[:tp].reshape(bp, sp, d), y[tp:].reshape(bs, ss, d),
            k_sb[:tp].reshape(1, bp, sp, SB_HEADS, HEAD), v_sb[:tp].reshape(1, bp, sp, SB_HEADS, HEAD),
            c_p[None], n_p[None], m_p[:, 0, :ML_HEADS][None],
            k_sb[tp:].reshape(1, bs, ss, SB_HEADS, HEAD), v_sb[tp:].reshape(1, bs, ss, SB_HEADS, HEAD),
            c_s[None], n_s[None], m_s[:, 0, :ML_HEADS][None])
```
